```python
import math
import jax
import jax.numpy as jnp
from jax import lax
import numpy as np

D_MODEL = 1024
BATCH = 4
SEQ = 4096
DEPTH = 4

N_MIXERS = 2
EPS = 1e-6

GDN_HEADS = 8
GDN_DK = D_MODEL // GDN_HEADS
GDN_DV = D_MODEL // GDN_HEADS
GDN_CONV = 4
GDN_CHUNK = 64

DSW_PATTERNS = ((128, 1), (512, 4), (2048, 16))
DSW_GROUPS = len(DSW_PATTERNS)
DSW_HEADS = 8
DSW_HEAD_DIM = D_MODEL // DSW_HEADS
DSW_BLOCK = 128
ROPE_THETA = 500000.0
ROPE_DIM = DSW_HEAD_DIM // 4

FFN_HIDDEN = ((8 * D_MODEL + 2) // 3 + 255) // 256 * 256

kernel_name = 'hybrid_gdn_dilated_swa_adaln_block'


def rms_norm(x, g):
    xf = x.astype(jnp.float32)
    y = xf * lax.rsqrt(jnp.mean(xf * xf, axis=-1, keepdims=True) + EPS)
    return (y * g.astype(jnp.float32)).astype(x.dtype)


def l2_norm(x):
    xf = x.astype(jnp.float32)
    return xf * lax.rsqrt(jnp.sum(xf * xf, axis=-1, keepdims=True) + EPS)


def partial_rope(x, pos):
    half = ROPE_DIM // 2
    inv = jnp.exp(-math.log(ROPE_THETA) * (2.0 * jnp.arange(half, dtype=jnp.float32) / ROPE_DIM))
    ang = pos.astype(jnp.float32)[:, None] * inv[None, :]
    cos = jnp.cos(ang)[None, :, None, :]
    sin = jnp.sin(ang)[None, :, None, :]
    x1 = x[..., :half]
    x2 = x[..., half:ROPE_DIM]
    return jnp.concatenate([x1 * cos - x2 * sin, x2 * cos + x1 * sin, x[..., ROPE_DIM:]], axis=-1)


def causal_dwconv(x, w):
    K, C = w.shape
    return lax.conv_general_dilated(
        x, w[:, None, :].astype(x.dtype), window_strides=(1,), padding=((K - 1, 0),),
        dimension_numbers=('NWC', 'WIO', 'NWC'), feature_group_count=C)


def chunk_gated_delta_rule(q, k, v, g, beta):
    B, T, H, dk = q.shape
    dv = v.shape[-1]
    C = GDN_CHUNK
    N = T // C

    def to_chunks(t):
        return jnp.moveaxis(t.reshape(B, N, C, H, t.shape[-1]), 3, 1)

    q, k, v = to_chunks(q), to_chunks(k), to_chunks(v)
    g = jnp.moveaxis(g.reshape(B, N, C, H), 3, 1)
    beta = jnp.moveaxis(beta.reshape(B, N, C, H), 3, 1)
    g = jnp.cumsum(g, axis=-1)
    causal = jnp.tril(jnp.ones((C, C), dtype=bool))
    strict = jnp.tril(jnp.ones((C, C), dtype=bool), -1)
    decay = jnp.exp(jnp.where(causal, g[..., :, None] - g[..., None, :], -jnp.inf))
    k_beta = k * beta[..., None]
    v_beta = v * beta[..., None]
    L = jnp.einsum('bhnid,bhnjd->bhnij', k_beta, k) * decay
    A = jnp.eye(C, dtype=jnp.float32) + jnp.where(strict, L, 0.0)
    rhs = jnp.concatenate([v_beta, k_beta * jnp.exp(g)[..., None]], axis=-1)
    sol = lax.linalg.triangular_solve(A, rhs, left_side=True, lower=True, unit_diagonal=True)
    u = sol[..., :dv]
    w = sol[..., dv:]
    attn_intra = jnp.where(causal, jnp.einsum('bhnid,bhnjd->bhnij', q, k) * decay, 0.0)
    q_decay = q * jnp.exp(g)[..., None]
    k_tail = k * jnp.exp(g[..., -1:] - g)[..., None]
    g_last = jnp.exp(g[..., -1])

    def step(S, xs):
        qd, kt, a_in, u_c, w_c, gl = xs
        v_new = u_c - jnp.einsum('bhcd,bhde->bhce', w_c, S)
        o = jnp.einsum('bhcd,bhde->bhce', qd, S) + jnp.einsum('bhij,bhje->bhie', a_in, v_new)
        S = S * gl[..., None, None] + jnp.einsum('bhcd,bhce->bhde', kt, v_new)
        return S, o

    xs = tuple(jnp.moveaxis(t, 2, 0) for t in (q_decay, k_tail, attn_intra, u, w, g_last))
    S0 = jnp.zeros((B, H, dk, dv), jnp.float32)
    _, o = lax.scan(step, S0, xs)
    o = jnp.moveaxis(o, 0, 2)
    return jnp.moveaxis(o, 1, 3).reshape(B, T, H, dv)


def gated_deltanet(h, w_in, conv_w, A_log, dt_bias, norm_g, w_out):
    B, T, _ = h.shape
    H, dk, dv = GDN_HEADS, GDN_DK, GDN_DV
    nq, nv = H * dk, H * dv
    proj = h @ w_in
    qkv = jax.nn.silu(causal_dwconv(proj[..., :2 * nq + nv], conv_w))
    z = proj[..., 2 * nq + nv:2 * nq + 2 * nv].reshape(B, T, H, dv)
    b = proj[..., 2 * nq + 2 * nv:2 * nq + 2 * nv + H].astype(jnp.float32)
    a = proj[..., 2 * nq + 2 * nv + H:].astype(jnp.float32)
    q = l2_norm(qkv[..., :nq].reshape(B, T, H, dk)) * (dk ** -0.5)
    k = l2_norm(qkv[..., nq:2 * nq].reshape(B, T, H, dk))
    v = qkv[..., 2 * nq:].reshape(B, T, H, dv).astype(jnp.float32)
    beta = jax.nn.sigmoid(b)
    g = -jnp.exp(A_log.astype(jnp.float32)) * jax.nn.softplus(a + dt_bias.astype(jnp.float32))
    o = chunk_gated_delta_rule(q, k, v, g, beta)
    o = rms_norm(o, norm_g) * jax.nn.silu(z.astype(jnp.float32))
    return o.reshape(B, T, nv).astype(h.dtype) @ w_out


def dilated_group_attention(q, k, v, dilation, span):
    B, T, H, dh = q.shape
    d = dilation
    Ts = T // d
    P = DSW_BLOCK
    nb = -(-Ts // P)
    Lp = nb * P

    def streams(t):
        t = jnp.swapaxes(t.reshape(B, Ts, d, H, dh), 1, 2)
        t = jnp.pad(t, ((0, 0), (0, 0), (0, Lp - Ts), (0, 0), (0, 0)))
        return t.reshape(B, d, nb, P, H, dh)

    def with_prev(t):
        prev = jnp.pad(t, ((0, 0), (0, 0), (1, 0), (0, 0), (0, 0), (0, 0)))[:, :, :-1]
        return jnp.concatenate([prev, t], axis=3)

    qs = streams(q)
    kb = with_prev(streams(k))
    vb = with_prev(streams(v))
    s = jnp.einsum('bgnqhe,bgnkhe->bgnhqk', qs, kb) * (dh ** -0.5)
    qi = jnp.arange(P)[:, None] + P
    ki = jnp.arange(2 * P)[None, :]
    rel = qi - ki
    band = (rel >= 0) & (rel <= span)
    blk = jnp.arange(nb)[:, None, None]
    valid = band[None] & ((blk > 0) | (ki[None] >= P))
    s = jnp.where(valid[:, None], s, -jnp.inf)
    lse = jax.nn.logsumexp(s, axis=-1)
    p = jnp.exp(s - lse[..., None])
    o = jnp.einsum('bgnhqk,bgnkhe->bgnqhe', p, vb)
    o = o.reshape(B, d, Lp, H, dh)[:, :, :Ts]
    o = jnp.swapaxes(o, 1, 2).reshape(B, T, H, dh)
    lse = jnp.swapaxes(lse, 3, 4).reshape(B, d, Lp, H)[:, :, :Ts]
    lse = jnp.swapaxes(lse, 1, 2).reshape(B, T, H)
    return o, lse


def dilated_window_attention(h, w_in, q_norm_g, k_norm_g, w_out):
    B, T, _ = h.shape
    G, H, dh = DSW_GROUPS, DSW_HEADS, DSW_HEAD_DIM
    proj = (h @ w_in).reshape(B, T, G, 3, H, dh)
    pos = jnp.arange(T)
    outs = []
    lses = []
    for gi, (window, dilation) in enumerate(DSW_PATTERNS):
        q = partial_rope(rms_norm(proj[:, :, gi, 0].astype(jnp.float32), q_norm_g[gi]), pos)
        k = partial_rope(rms_norm(proj[:, :, gi, 1].astype(jnp.float32), k_norm_g[gi]), pos)
        v = proj[:, :, gi, 2].astype(jnp.float32)
        o, lse = dilated_group_attention(q, k, v, dilation, window // dilation)
        outs.append(o)
        lses.append(lse)
    wts = jax.nn.softmax(jnp.stack(lses, axis=0), axis=0)
    o = jnp.sum(wts[..., None] * jnp.stack(outs, axis=0), axis=0)
    return o.reshape(B, T, H * dh).astype(h.dtype) @ w_out


def swiglu(h, w_gate_up, w_down):
    gate, up = jnp.split(h @ w_gate_up, 2, axis=-1)
    return (jax.nn.silu(gate) * up) @ w_down


def setup_inputs(seed: int = 0) -> dict:
    key = jax.random.key(seed)
    ks = jax.random.split(key, 20)
    D = D_MODEL
    f32 = jnp.float32
    nA = (DEPTH + N_MIXERS - 1) // N_MIXERS
    nB = DEPTH // N_MIXERS
    H = GDN_HEADS
    gdn_in = 2 * H * GDN_DK + 2 * H * GDN_DV + 2 * H
    gdn_conv_ch = 2 * H * GDN_DK + H * GDN_DV
    dsw_in = DSW_GROUPS * 3 * DSW_HEADS * DSW_HEAD_DIM

    def nrm(k, shape, fan_in):
        return jax.random.normal(k, shape, f32) * (fan_in ** -0.5)

    def gain(k, shape):
        return 1.0 + 0.02 * jax.random.normal(k, shape, f32)

    dt = jnp.exp(jax.random.uniform(ks[9], (nA, H), f32, math.log(1e-3), math.log(1e-1)))
    return {
        'x': jax.random.normal(ks[0], (BATCH, SEQ, D), f32),
        'c': jax.random.normal(ks[1], (BATCH, D), f32),
        'mod_w': nrm(ks[2], (DEPTH, D, 6 * D), D),
        'mod_b': 0.02 * jax.random.normal(ks[3], (DEPTH, 6 * D), f32),
        'mix_norm_g': gain(ks[4], (DEPTH, D)),
        'ffn_norm_g': gain(ks[5], (DEPTH, D)),
        'gdn_w_in': nrm(ks[6], (nA, D, gdn_in), D),
        'gdn_conv_w': nrm(ks[7], (nA, GDN_CONV, gdn_conv_ch), GDN_CONV),
        'gdn_A_log': jnp.log(jax.random.uniform(ks[8], (nA, H), f32, 1.0, 16.0)),
        'gdn_dt_bias': dt + jnp.log(-jnp.expm1(-dt)),
        'gdn_norm_g': gain(ks[10], (nA, GDN_DV)),
        'gdn_w_out': nrm(ks[11], (nA, H * GDN_DV, D), H * GDN_DV),
        'dsw_w_in': nrm(ks[12], (nB, D, dsw_in), D),
        'dsw_q_norm_g': gain(ks[13], (nB, DSW_GROUPS, DSW_HEAD_DIM)),
        'dsw_k_norm_g': gain(ks[14], (nB, DSW_GROUPS, DSW_HEAD_DIM)),
        'dsw_w_out': nrm(ks[15], (nB, DSW_HEADS * DSW_HEAD_DIM, D), DSW_HEADS * DSW_HEAD_DIM),
        'ffn_w_gate_up': nrm(ks[16], (DEPTH, D, 2 * FFN_HIDDEN), D),
        'ffn_w_down': nrm(ks[17], (DEPTH, FFN_HIDDEN, D), FFN_HIDDEN),
    }


def reference(x, c, mod_w, mod_b, mix_norm_g, ffn_norm_g, gdn_w_in, gdn_conv_w, gdn_A_log,
              gdn_dt_bias, gdn_norm_g, gdn_w_out, dsw_w_in, dsw_q_norm_g, dsw_k_norm_g,
              dsw_w_out, ffn_w_gate_up, ffn_w_down):
    cond = jax.nn.silu(c)
    for layer in range(DEPTH):
        mod = cond @ mod_w[layer] + mod_b[layer]
        sh1, sc1, g1, sh2, sc2, g2 = [m[:, None, :] for m in jnp.split(mod, 6, axis=-1)]
        h = rms_norm(x, mix_norm_g[layer]) * (1 + sc1) + sh1
        j = layer // N_MIXERS
        if layer % N_MIXERS == 0:
            y = gated_deltanet(h, gdn_w_in[j], gdn_conv_w[j], gdn_A_log[j], gdn_dt_bias[j],
                               gdn_norm_g[j], gdn_w_out[j])
        else:
            y = dilated_window_attention(h, dsw_w_in[j], dsw_q_norm_g[j], dsw_k_norm_g[j],
                                         dsw_w_out[j])
        x = x + g1 * y
        h = rms_norm(x, ffn_norm_g[layer]) * (1 + sc2) + sh2
        x = x + g2 * swiglu(h, ffn_w_gate_up[layer], ffn_w_down[layer])
    return x
```

```python
import functools
import math

import jax
import jax.numpy as jnp
from jax import lax
from jax.experimental import pallas as pl
from jax.experimental.pallas import tpu as pltpu

EPS = 1e-6
N_MIXERS = 2
GDN_HEADS = 8
GDN_CONV = 4
GDN_CHUNK = 64
GDN_SOLVE_BLOCK = 16
DSW_PATTERNS = ((128, 1), (512, 4), (2048, 16))
DSW_HEADS = 8
DSW_BLOCK = 128
ROPE_THETA = 500000.0

LANES = 128
SUBLANES = 8
VMEM_LIMIT_BYTES = 56 * 1024 * 1024

ROW_TILE = 1024
GDN_TIME_BLOCK = 512

f32 = jnp.float32
bf16 = jnp.bfloat16


def _params(*sem):
    return pltpu.CompilerParams(dimension_semantics=sem, vmem_limit_bytes=VMEM_LIMIT_BYTES)


def _dot(a, b):
    return jnp.dot(a, b, preferred_element_type=f32)


def _dot_nt(a, b):
    return lax.dot_general(a, b, (((1,), (1,)), ((), ())), preferred_element_type=f32)


def _dot_tn(a, b):
    return lax.dot_general(a, b, (((0,), (0,)), ((), ())), preferred_element_type=f32)


def _split_bf16(a):
    hi = a.astype(bf16)
    return hi, (a - hi.astype(f32)).astype(bf16)


def _dot3(a, b):
    ah, al = _split_bf16(a)
    bh, bl = _split_bf16(b)
    return _dot(ah, bh) + (_dot(ah, bl) + _dot(al, bh))


def _silu(x):
    return x * jax.nn.sigmoid(x)


def _softplus(x):
    return jnp.maximum(x, 0.0) + jnp.log1p(jnp.exp(-jnp.abs(x)))


def _norm_modulate(x, gain, shift, scale):
    ms = jnp.mean(x * x, axis=-1, keepdims=True)
    y = x * lax.rsqrt(ms + EPS) * gain
    return y * (1.0 + scale) + shift


def _mod_body(c_ref, w_ref, b_ref, o_ref):
    c = c_ref[...]
    cond = _silu(c).astype(bf16)
    o_ref[...] = _dot(cond, w_ref[...].astype(bf16)) + b_ref[...]


def _modulation(c_pad, mod_w, mod_b):
    depth, d, n = mod_w.shape
    rows = c_pad.shape[0]
    tn = 1536
    return pl.pallas_call(
        _mod_body,
        grid=(depth, n // tn),
        in_specs=[
            pl.BlockSpec((rows, d), lambda l, j: (0, 0)),
            pl.BlockSpec((None, d, tn), lambda l, j: (l, 0, j)),
            pl.BlockSpec((None, 1, tn), lambda l, j: (l, 0, j)),
        ],
        out_specs=pl.BlockSpec((None, rows, tn), lambda l, j: (l, 0, j)),
        out_shape=jax.ShapeDtypeStruct((depth, rows, n), f32),
        compiler_params=_params("arbitrary", "arbitrary"),
        name="modulation",
    )(c_pad, mod_w, mod_b.reshape(depth, 1, n))


def _mod_spec(idx, tiles_per_seq, d):
    return pl.BlockSpec((None, None, 1, d), lambda i, *_: (i // tiles_per_seq, idx, 0, 0))


def _gdn_in_body(x_ref, ng_ref, sh_ref, sc_ref, w_ref, wba_ref, cw_ref, gp_ref,
                 out_ref, ba_ref, h_scr, ext_scr, tail_scr, *, tiles_per_seq, tm, dk):
    i = pl.program_id(0)
    j = pl.program_id(1)

    @pl.when(j == 0)
    def _():
        h = _norm_modulate(x_ref[...], ng_ref[...], sh_ref[...], sc_ref[...])
        hb = h.astype(bf16)
        h_scr[...] = hb
        ba = _dot(hb, wba_ref[...])
        lane = lax.broadcasted_iota(jnp.int32, ba.shape, 1)
        beta = jax.nn.sigmoid(ba)
        g = -jnp.exp(gp_ref[0:1, :]) * _softplus(ba + gp_ref[1:2, :])
        ba_ref[...] = jnp.where(lane < GDN_HEADS, beta, g)

    y = _dot(h_scr[...], w_ref[...])

    @pl.when(j == 3)
    def _():
        out_ref[...] = y.astype(bf16)

    @pl.when(j < 3)
    def _():
        jj = jnp.minimum(j, 2)
        first = (i % tiles_per_seq) == 0

        @pl.when(first)
        def _():
            ext_scr[0:SUBLANES, :] = jnp.zeros((SUBLANES, y.shape[1]), f32)

        @pl.when(jnp.logical_not(first))
        def _():
            ext_scr[0:SUBLANES, :] = tail_scr[jj]

        ext_scr[SUBLANES:, :] = y
        tail_scr[jj] = y[tm - SUBLANES:, :]
        cw = cw_ref[...]
        acc = cw[GDN_CONV - 1:GDN_CONV, :] * y
        for kk in range(GDN_CONV - 1):
            start = SUBLANES - (GDN_CONV - 1) + kk
            acc = acc + cw[kk:kk + 1, :] * ext_scr[pl.ds(start, tm), :]
        a = _silu(acc)

        @pl.when(j == 2)
        def _():
            out_ref[...] = a.astype(bf16)

        @pl.when(j < 2)
        def _():
            post = jnp.where(j == 0, dk ** -0.5, 1.0)
            for h in range(GDN_HEADS):
                sl = slice(h * dk, (h + 1) * dk)
                ah = a[:, sl]
                ss = jnp.sum(ah * ah, axis=-1, keepdims=True)
                out_ref[:, sl] = (ah * (lax.rsqrt(ss + EPS) * post)).astype(bf16)


def _gdn_in_proj(x, mod_l, norm_g, w4, wba, cw, gp, *, seq):
    m, d = x.shape
    tm = ROW_TILE
    tps = seq // tm
    n4 = w4.shape[1]
    tn = n4 // 4
    dk = tn // GDN_HEADS
    body = functools.partial(_gdn_in_body, tiles_per_seq=tps, tm=tm, dk=dk)
    return pl.pallas_call(
        body,
        grid=(m // tm, 4),
        in_specs=[
            pl.BlockSpec((tm, d), lambda i, j: (i, 0)),
            pl.BlockSpec((1, d), lambda i, j: (0, 0)),
            _mod_spec(0, tps, d),
            _mod_spec(1, tps, d),
            pl.BlockSpec((d, tn), lambda i, j: (0, j)),
            pl.BlockSpec((d, LANES), lambda i, j: (0, 0)),
            pl.BlockSpec((GDN_CONV, tn), lambda i, j: (0, j)),
            pl.BlockSpec((SUBLANES, LANES), lambda i, j: (0, 0)),
        ],
        out_specs=[
            pl.BlockSpec((tm, tn), lambda i, j: (i, j)),
            pl.BlockSpec((tm, LANES), lambda i, j: (i, 0)),
        ],
        out_shape=[
            jax.ShapeDtypeStruct((m, n4), bf16),
            jax.ShapeDtypeStruct((m, LANES), f32),
        ],
        scratch_shapes=[
            pltpu.VMEM((tm, d), bf16),
            pltpu.VMEM((tm + SUBLANES, tn), f32),
            pltpu.VMEM((3, SUBLANES, tn), f32),
        ],
        compiler_params=_params("arbitrary", "arbitrary"),
        name="gdn_in_proj",
    )(x, norm_g, mod_l, mod_l, w4, wba, cw, gp)


def _gdn_core_body(q_ref, k_ref, v_ref, z_ref, ba_ref, ng_ref, o_ref,
                   s_scr, u_scr, w_scr, qd_scr, kt_scr, a_scr, gl_scr, *, tb, dk):
    c_len = GDN_CHUNK
    n_chunks = tb // c_len
    heads = GDN_HEADS

    @pl.when(pl.program_id(1) == 0)
    def _():
        s_scr[...] = jnp.zeros(s_scr.shape, f32)

    row = lax.broadcasted_iota(jnp.int32, (c_len, c_len), 0)
    col = lax.broadcasted_iota(jnp.int32, (c_len, c_len), 1)
    causal = row >= col
    strict = row > col
    tril_f = causal.astype(f32)
    eye_f = (row == col).astype(f32)
    diag_blk = (row // GDN_SOLVE_BLOCK) == (col // GDN_SOLVE_BLOCK)

    def prep(c, carry):
        r0 = pl.multiple_of(c * c_len, c_len)
        rows = pl.ds(r0, c_len)
        ba = ba_ref[rows, :]
        gc_all = jnp.dot(tril_f, ba, precision=lax.Precision.HIGHEST,
                         preferred_element_type=f32)
        gc_t = gc_all.T
        for h in range(heads):
            sl = slice(h * dk, (h + 1) * dk)
            q = q_ref[rows, sl]
            k = k_ref[rows, sl]
            v = v_ref[rows, sl]
            beta = ba[:, h:h + 1]
            gc = gc_all[:, heads + h:heads + h + 1]
            gr = gc_t[heads + h:heads + h + 1, :]
            g_last = gc_all[c_len - 1:c_len, heads + h:heads + h + 1]
            decay = jnp.where(causal, jnp.exp(jnp.minimum(gc - gr, 0.0)), 0.0)
            kf = k.astype(f32)
            kb = kf * beta
            kk = _dot_nt(kb.astype(bf16), k)
            x = jnp.where(strict, -(kk * decay), 0.0)
            x_d = jnp.where(diag_blk, x, 0.0)
            d_inv = eye_f + x_d
            xp = x_d
            for _ in range(int(math.log2(GDN_SOLVE_BLOCK)) - 1):
                xp = _dot3(xp, xp)
                d_inv = d_inv + _dot3(d_inv, xp)
            mn = _dot3(d_inv, x - x_d)
            o_inv = eye_f + mn
            mp = mn
            for _ in range(int(math.log2(c_len // GDN_SOLVE_BLOCK)) - 1):
                mp = _dot3(mp, mp)
                o_inv = o_inv + _dot3(o_inv, mp)
            t_inv = _dot3(o_inv, d_inv)
            eg = jnp.exp(gc)
            rhs = jnp.concatenate([v.astype(f32) * beta, kb * eg], axis=-1)
            sol = _dot3(t_inv, rhs)
            a_in = jnp.where(causal, _dot_nt(q, k) * decay, 0.0)
            u_scr[h, rows, :] = sol[:, :dk]
            w_scr[h, rows, :] = sol[:, dk:].astype(bf16)
            qd_scr[h, rows, :] = (q.astype(f32) * eg).astype(bf16)
            kt_scr[h, rows, :] = (kf * jnp.exp(g_last - gc)).astype(bf16)
            a_scr[h, rows, :] = a_in.astype(bf16)
            g8 = pl.multiple_of(c * SUBLANES, SUBLANES)
            gl_scr[h, pl.ds(g8, SUBLANES), :] = jnp.broadcast_to(jnp.exp(g_last), (SUBLANES, dk))
        return carry

    lax.fori_loop(0, n_chunks, prep, 0)

    ng = ng_ref[...]

    def scan(c, carry):
        r0 = pl.multiple_of(c * c_len, c_len)
        rows = pl.ds(r0, c_len)
        g8 = pl.multiple_of(c * SUBLANES, SUBLANES)
        for h in range(heads):
            sl = slice(h * dk, (h + 1) * dk)
            s = s_scr[h]
            sb = s.astype(bf16)
            ws = _dot(w_scr[h, rows, :], sb)
            qs = _dot(qd_scr[h, rows, :], sb)
            v_new = (u_scr[h, rows, :] - ws).astype(bf16)
            o = qs + _dot(a_scr[h, rows, :], v_new)
            gl = gl_scr[h, pl.ds(g8, 1), :]
            s_scr[h] = s * gl + _dot_tn(kt_scr[h, rows, :], v_new)
            z = z_ref[rows, sl].astype(f32)
            ms = jnp.mean(o * o, axis=-1, keepdims=True)
            o_ref[rows, sl] = (o * lax.rsqrt(ms + EPS) * ng * _silu(z)).astype(bf16)
        return carry

    lax.fori_loop(0, n_chunks, scan, 0)


def _gdn_core(qkvz, ba, norm_g, *, batch, seq):
    m = qkvz.shape[0]
    d = qkvz.shape[1] // 4
    dk = d // GDN_HEADS
    tb = GDN_TIME_BLOCK
    nt = seq // tb
    n_chunks = tb // GDN_CHUNK

    def col(jc):
        return pl.BlockSpec((tb, d), lambda b, t: (b * nt + t, jc))

    body = functools.partial(_gdn_core_body, tb=tb, dk=dk)
    return pl.pallas_call(
        body,
        grid=(batch, nt),
        in_specs=[col(0), col(1), col(2), col(3),
                  pl.BlockSpec((tb, LANES), lambda b, t: (b * nt + t, 0)),
                  pl.BlockSpec((1, dk), lambda b, t: (0, 0))],
        out_specs=pl.BlockSpec((tb, d), lambda b, t: (b * nt + t, 0)),
        out_shape=jax.ShapeDtypeStruct((m, d), bf16),
        scratch_shapes=[
            pltpu.VMEM((GDN_HEADS, dk, dk), f32),
            pltpu.VMEM((GDN_HEADS, tb, dk), f32),
            pltpu.VMEM((GDN_HEADS, tb, dk), bf16),
            pltpu.VMEM((GDN_HEADS, tb, dk), bf16),
            pltpu.VMEM((GDN_HEADS, tb, dk), bf16),
            pltpu.VMEM((GDN_HEADS, tb, GDN_CHUNK), bf16),
            pltpu.VMEM((GDN_HEADS, n_chunks * SUBLANES, dk), f32),
        ],
        compiler_params=_params("arbitrary", "arbitrary"),
        name="gdn_core",
    )(qkvz, qkvz, qkvz, qkvz, ba, norm_g)


def _out_proj_body(x_ref, o_ref, w_ref, g_ref, out_ref):
    out_ref[...] = x_ref[...] + g_ref[...] * _dot(o_ref[...], w_ref[...])


def _out_proj(x, o, w, mod_l, gate_idx, *, seq):
    m, d = x.shape
    tm = ROW_TILE
    tps = seq // tm
    return pl.pallas_call(
        _out_proj_body,
        grid=(m // tm,),
        in_specs=[
            pl.BlockSpec((tm, d), lambda i: (i, 0)),
            pl.BlockSpec((tm, d), lambda i: (i, 0)),
            pl.BlockSpec((d, d), lambda i: (0, 0)),
            _mod_spec(gate_idx, tps, d),
        ],
        out_specs=pl.BlockSpec((tm, d), lambda i: (i, 0)),
        out_shape=jax.ShapeDtypeStruct((m, d), f32),
        compiler_params=_params("arbitrary"),
        name="out_proj",
    )(x, o, w, mod_l)


def _dsw_in_body(x_ref, ng_ref, sh_ref, sc_ref, w_ref, gn_ref, cos_ref, sa_ref, sb_ref,
                 out_ref, h_scr, *, dh):
    j = pl.program_id(1)

    @pl.when(j == 0)
    def _():
        h = _norm_modulate(x_ref[...], ng_ref[...], sh_ref[...], sc_ref[...])
        h_scr[...] = h.astype(bf16)

    y = _dot(h_scr[...], w_ref[...])
    which = j % 3

    @pl.when(which == 2)
    def _():
        out_ref[...] = y.astype(bf16)

    @pl.when(which < 2)
    def _():
        cos_t = cos_ref[...]
        sin_a = sa_ref[...]
        sin_b = sb_ref[...]
        half = (dh // 4) // 2
        for h in range(DSW_HEADS):
            sl = slice(h * dh, (h + 1) * dh)
            yh = y[:, sl]
            ms = jnp.mean(yh * yh, axis=-1, keepdims=True)
            yn = yh * lax.rsqrt(ms + EPS) * gn_ref[:, sl]
            rot = (yn * cos_t + pltpu.roll(yn, dh - half, 1) * sin_a
                   + pltpu.roll(yn, half, 1) * sin_b)
            out_ref[:, sl] = rot.astype(bf16)


def _dsw_in_proj(x, mod_l, norm_g, w, gn, cos_t, sin_a, sin_b, *, seq):
    m, d = x.shape
    tm = ROW_TILE
    tps = seq // tm
    n = w.shape[1]
    tn = d
    dh = d // DSW_HEADS
    body = functools.partial(_dsw_in_body, dh=dh)
    rope_spec = pl.BlockSpec((tm, dh), lambda i, j: (i % tps, 0))
    return pl.pallas_call(
        body,
        grid=(m // tm, n // tn),
        in_specs=[
            pl.BlockSpec((tm, d), lambda i, j: (i, 0)),
            pl.BlockSpec((1, d), lambda i, j: (0, 0)),
            _mod_spec(0, tps, d),
            _mod_spec(1, tps, d),
            pl.BlockSpec((d, tn), lambda i, j: (0, j)),
            pl.BlockSpec((None, 1, tn), lambda i, j: (j, 0, 0)),
            rope_spec, rope_spec, rope_spec,
        ],
        out_specs=pl.BlockSpec((tm, tn), lambda i, j: (i, j)),
        out_shape=jax.ShapeDtypeStruct((m, n), bf16),
        scratch_shapes=[pltpu.VMEM((tm, d), bf16)],
        compiler_params=_params("arbitrary", "arbitrary"),
        name="dsw_in_proj",
    )(x, norm_g, mod_l, mod_l, w, gn, cos_t, sin_a, sin_b)


def _dsw_attn_body(q_ref, kp_ref, kc_ref, vp_ref, vc_ref, o_ref, lse_ref, *, dh, span):
    blk = pl.program_id(2)
    p_len = DSW_BLOCK
    qi = lax.broadcasted_iota(jnp.int32, (p_len, 2 * p_len), 0)
    ki = lax.broadcasted_iota(jnp.int32, (p_len, 2 * p_len), 1)
    rel = qi + p_len - ki
    valid = (rel >= 0) & (rel <= span) & ((blk > 0) | (ki >= p_len))
    lane = lax.broadcasted_iota(jnp.int32, (p_len, LANES), 1)
    lse_tile = jnp.zeros((p_len, LANES), f32)
    for h in range(DSW_HEADS):
        sl = slice(h * dh, (h + 1) * dh)
        q = q_ref[:, sl]
        kcat = jnp.concatenate([kp_ref[:, sl], kc_ref[:, sl]], axis=0)
        vcat = jnp.concatenate([vp_ref[:, sl], vc_ref[:, sl]], axis=0)
        s = jnp.where(valid, _dot_nt(q, kcat), -jnp.inf)
        mx = jnp.max(s, axis=-1, keepdims=True)
        p = jnp.exp(s - mx)
        l = jnp.sum(p, axis=-1, keepdims=True)
        o = _dot(p.astype(bf16), vcat) * (1.0 / l)
        o_ref[:, sl] = o.astype(bf16)
        lse_tile = jnp.where(lane == h, mx + jnp.log(l), lse_tile)
    lse_ref[...] = lse_tile


def _dsw_attention(proj, gi, window, dilation, *, batch, seq, d):
    m, n = proj.shape
    dh = d // DSW_HEADS
    ts = seq // dilation
    nb = ts // DSW_BLOCK
    span = window // dilation
    slabs = n // d
    view = proj.reshape(batch, ts, dilation * n)

    def in_spec(which, prev):
        def imap(b, r, blk):
            bb = jnp.maximum(blk - 1, 0) if prev else blk
            return (b, bb, r * slabs + gi * 3 + which)
        return pl.BlockSpec((None, DSW_BLOCK, d), imap)

    body = functools.partial(_dsw_attn_body, dh=dh, span=span)
    o, lse = pl.pallas_call(
        body,
        grid=(batch, dilation, nb),
        in_specs=[in_spec(0, False), in_spec(1, True), in_spec(1, False),
                  in_spec(2, True), in_spec(2, False)],
        out_specs=[
            pl.BlockSpec((None, DSW_BLOCK, d), lambda b, r, blk: (b, blk, r)),
            pl.BlockSpec((None, DSW_BLOCK, LANES), lambda b, r, blk: (b, blk, r)),
        ],
        out_shape=[
            jax.ShapeDtypeStruct((batch, ts, dilation * d), bf16),
            jax.ShapeDtypeStruct((batch, ts, dilation * LANES), f32),
        ],
        compiler_params=_params("arbitrary", "arbitrary", "arbitrary"),
        name=f"dsw_attn_d{dilation}",
    )(view, view, view, view, view)
    return o.reshape(m, d), lse.reshape(m, LANES)


def _dsw_out_body(x_ref, o0_ref, o1_ref, o2_ref, l0_ref, l1_ref, l2_ref, w_ref, g_ref,
                  out_ref, om_scr, *, dh):
    l0 = l0_ref[...]
    l1 = l1_ref[...]
    l2 = l2_ref[...]
    mx = jnp.maximum(jnp.maximum(l0, l1), l2)
    e0 = jnp.exp(l0 - mx)
    e1 = jnp.exp(l1 - mx)
    e2 = jnp.exp(l2 - mx)
    inv = 1.0 / (e0 + e1 + e2)
    w0 = e0 * inv
    w1 = e1 * inv
    w2 = e2 * inv
    for h in range(DSW_HEADS):
        sl = slice(h * dh, (h + 1) * dh)
        om = (w0[:, h:h + 1] * o0_ref[:, sl].astype(f32)
              + w1[:, h:h + 1] * o1_ref[:, sl].astype(f32)
              + w2[:, h:h + 1] * o2_ref[:, sl].astype(f32))
        om_scr[:, sl] = om.astype(bf16)
    out_ref[...] = x_ref[...] + g_ref[...] * _dot(om_scr[...], w_ref[...])


def _dsw_out_proj(x, outs, lses, w, mod_l, *, seq):
    m, d = x.shape
    tm = ROW_TILE
    tps = seq // tm
    dh = d // DSW_HEADS
    row = pl.BlockSpec((tm, d), lambda i: (i, 0))
    lrow = pl.BlockSpec((tm, LANES), lambda i: (i, 0))
    body = functools.partial(_dsw_out_body, dh=dh)
    return pl.pallas_call(
        body,
        grid=(m // tm,),
        in_specs=[row, row, row, row, lrow, lrow, lrow,
                  pl.BlockSpec((d, d), lambda i: (0, 0)),
                  _mod_spec(2, tps, d)],
        out_specs=row,
        out_shape=jax.ShapeDtypeStruct((m, d), f32),
        scratch_shapes=[pltpu.VMEM((tm, d), bf16)],
        compiler_params=_params("arbitrary"),
        name="dsw_out_proj",
    )(x, *outs, *lses, w, mod_l)


def _ffn_body(x_ref, ng_ref, sh_ref, sc_ref, g_ref, wg_ref, wu_ref, wd_ref, out_ref,
              h_scr, acc_scr):
    j = pl.program_id(1)

    @pl.when(j == 0)
    def _():
        h = _norm_modulate(x_ref[...], ng_ref[...], sh_ref[...], sc_ref[...])
        h_scr[...] = h.astype(bf16)
        acc_scr[...] = jnp.zeros(acc_scr.shape, f32)

    hb = h_scr[...]
    gate = _dot(hb, wg_ref[...])
    up = _dot(hb, wu_ref[...])
    act = (_silu(gate) * up).astype(bf16)
    acc_scr[...] += _dot(act, wd_ref[...])

    @pl.when(j == pl.num_programs(1) - 1)
    def _():
        out_ref[...] = x_ref[...] + g_ref[...] * acc_scr[...]


def _ffn(x, mod_l, norm_g, w_gate_up, w_down, *, seq):
    m, d = x.shape
    tm = ROW_TILE
    tps = seq // tm
    hidden = w_down.shape[0]
    th = 256
    nh = hidden // th
    return pl.pallas_call(
        _ffn_body,
        grid=(m // tm, nh),
        in_specs=[
            pl.BlockSpec((tm, d), lambda i, j: (i, 0)),
            pl.BlockSpec((1, d), lambda i, j: (0, 0)),
            _mod_spec(3, tps, d),
            _mod_spec(4, tps, d),
            _mod_spec(5, tps, d),
            pl.BlockSpec((d, th), lambda i, j: (0, j)),
            pl.BlockSpec((d, th), lambda i, j: (0, j + nh)),
            pl.BlockSpec((th, d), lambda i, j: (j, 0)),
        ],
        out_specs=pl.BlockSpec((tm, d), lambda i, j: (i, 0)),
        out_shape=jax.ShapeDtypeStruct((m, d), f32),
        scratch_shapes=[pltpu.VMEM((tm, d), bf16), pltpu.VMEM((tm, d), f32)],
        compiler_params=_params("arbitrary", "arbitrary"),
        name="ffn",
    )(x, norm_g, mod_l, mod_l, mod_l, w_gate_up, w_gate_up, w_down)


def _rope_tables(seq, dh):
    rope_dim = dh // 4
    half = rope_dim // 2
    inv = jnp.exp(-math.log(ROPE_THETA) * (2.0 * jnp.arange(half, dtype=f32) / rope_dim))
    ang = jnp.arange(seq, dtype=f32)[:, None] * inv[None, :]
    cos = jnp.cos(ang)
    sin = jnp.sin(ang)
    ones = jnp.ones((seq, dh - rope_dim), f32)
    zeros_h = jnp.zeros((seq, half), f32)
    zeros_r = jnp.zeros((seq, dh - rope_dim), f32)
    cos_t = jnp.concatenate([cos, cos, ones], axis=-1)
    sin_a = jnp.concatenate([-sin, zeros_h, zeros_r], axis=-1)
    sin_b = jnp.concatenate([zeros_h, sin, zeros_r], axis=-1)
    return cos_t, sin_a, sin_b


def kernel(x, c, mod_w, mod_b, mix_norm_g, ffn_norm_g, gdn_w_in, gdn_conv_w, gdn_A_log,
           gdn_dt_bias, gdn_norm_g, gdn_w_out, dsw_w_in, dsw_q_norm_g, dsw_k_norm_g,
           dsw_w_out, ffn_w_gate_up, ffn_w_down):
    batch, seq, d = x.shape
    depth = mod_w.shape[0]
    m = batch * seq
    heads = GDN_HEADS
    dh = d // DSW_HEADS
    assert seq % ROW_TILE == 0 and seq % GDN_TIME_BLOCK == 0 and d == heads * LANES

    mod_rows = 16
    c_pad = jnp.zeros((mod_rows, d), f32).at[:batch].set(c)
    mod = _modulation(c_pad, mod_w, mod_b)[:, :batch]
    mod = mod.reshape(depth, batch, 6, 1, d)

    cos_t, sin_a, sin_b = _rope_tables(seq, dh)
    xf = x.reshape(m, d)

    for layer in range(depth):
        mod_l = mod[layer]
        j = layer // N_MIXERS
        norm_g = mix_norm_g[layer].reshape(1, d)
        if layer % N_MIXERS == 0:
            n_conv = 3 * d
            w_in = gdn_w_in[j]
            w4 = w_in[:, :4 * d].astype(bf16)
            wba = jnp.zeros((d, LANES), f32).at[:, :2 * heads].set(w_in[:, 4 * d:]).astype(bf16)
            cw = jnp.zeros((GDN_CONV, 4 * d), f32).at[:, :n_conv].set(gdn_conv_w[j])
            gp = jnp.zeros((SUBLANES, LANES), f32)
            gp = gp.at[0, heads:2 * heads].set(gdn_A_log[j]).at[1, heads:2 * heads].set(gdn_dt_bias[j])
            qkvz, ba = _gdn_in_proj(xf, mod_l, norm_g, w4, wba, cw, gp, seq=seq)
            o = _gdn_core(qkvz, ba, gdn_norm_g[j].reshape(1, -1), batch=batch, seq=seq)
            xf = _out_proj(xf, o, gdn_w_out[j].astype(bf16), mod_l, 2, seq=seq)
        else:
            w_in = dsw_w_in[j].astype(bf16)
            gains = []
            for gi in range(len(DSW_PATTERNS)):
                gains.append(jnp.tile(dsw_q_norm_g[j, gi] * (dh ** -0.5), DSW_HEADS))
                gains.append(jnp.tile(dsw_k_norm_g[j, gi], DSW_HEADS))
                gains.append(jnp.ones((d,), f32))
            gn = jnp.stack(gains).reshape(3 * len(DSW_PATTERNS), 1, d)
            proj = _dsw_in_proj(xf, mod_l, norm_g, w_in, gn, cos_t, sin_a, sin_b, seq=seq)
            outs, lses = [], []
            for gi, (window, dilation) in enumerate(DSW_PATTERNS):
                o_g, lse_g = _dsw_attention(proj, gi, window, dilation, batch=batch, seq=seq, d=d)
                outs.append(o_g)
                lses.append(lse_g)
            xf = _dsw_out_proj(xf, outs, lses, dsw_w_out[j].astype(bf16), mod_l, seq=seq)
        xf = _ffn(xf, mod_l, ffn_norm_g[layer].reshape(1, d), ffn_w_gate_up[layer].astype(bf16),
                  ffn_w_down[layer].astype(bf16), seq=seq)
    return xf.reshape(batch, seq, d)
```

```python
import functools
import math

import jax
import jax.numpy as jnp
from jax import lax
from jax.experimental import pallas as pl
from jax.experimental.pallas import tpu as pltpu

EPS = 1e-6
N_MIXERS = 2
GDN_HEADS = 8
GDN_CONV = 4
GDN_CHUNK = 64
GDN_SOLVE_BLOCK = 16
DSW_PATTERNS = ((128, 1), (512, 4), (2048, 16))
DSW_HEADS = 8
DSW_BLOCK = 128
ROPE_THETA = 500000.0

LANES = 128
SUBLANES = 8
VMEM_LIMIT_BYTES = 56 * 1024 * 1024

ROW_TILE = 1024
DSW_OUT_ROW_TILE = 512
GDN_TIME_BLOCK = 512

f32 = jnp.float32
bf16 = jnp.bfloat16


def _params(*sem):
    return pltpu.CompilerParams(dimension_semantics=sem, vmem_limit_bytes=VMEM_LIMIT_BYTES)


def _dot(a, b):
    return jnp.dot(a, b, preferred_element_type=f32)


def _dot_nt(a, b):
    return lax.dot_general(a, b, (((1,), (1,)), ((), ())), preferred_element_type=f32)


def _dot_tn(a, b):
    return lax.dot_general(a, b, (((0,), (0,)), ((), ())), preferred_element_type=f32)


def _split_bf16(a):
    hi = a.astype(bf16)
    return hi, (a - hi.astype(f32)).astype(bf16)


def _dot3(a, b):
    ah, al = _split_bf16(a)
    bh, bl = _split_bf16(b)
    return _dot(ah, bh) + (_dot(ah, bl) + _dot(al, bh))


def _silu(x):
    return x * jax.nn.sigmoid(x)


def _softplus(x):
    return jnp.maximum(x, 0.0) + jnp.log1p(jnp.exp(-jnp.abs(x)))


def _norm_modulate(x, gain, shift, scale):
    ms = jnp.mean(x * x, axis=-1, keepdims=True)
    y = x * lax.rsqrt(ms + EPS) * gain
    return y * (1.0 + scale) + shift


def _mod_body(c_ref, w_ref, b_ref, o_ref):
    c = c_ref[...]
    cond = _silu(c).astype(bf16)
    o_ref[...] = _dot(cond, w_ref[...].astype(bf16)) + b_ref[...]


def _modulation(c_pad, mod_w, mod_b):
    depth, d, n = mod_w.shape
    rows = c_pad.shape[0]
    tn = 1536
    return pl.pallas_call(
        _mod_body,
        grid=(depth, n // tn),
        in_specs=[
            pl.BlockSpec((rows, d), lambda l, j: (0, 0)),
            pl.BlockSpec((None, d, tn), lambda l, j: (l, 0, j)),
            pl.BlockSpec((None, 1, tn), lambda l, j: (l, 0, j)),
        ],
        out_specs=pl.BlockSpec((None, rows, tn), lambda l, j: (l, 0, j)),
        out_shape=jax.ShapeDtypeStruct((depth, rows, n), f32),
        compiler_params=_params("arbitrary", "arbitrary"),
        name="modulation",
    )(c_pad, mod_w, mod_b.reshape(depth, 1, n))


def _mod_spec(idx, tiles_per_seq, d):
    return pl.BlockSpec((None, None, 1, d), lambda i, *_: (i // tiles_per_seq, idx, 0, 0))


def _gdn_in_body(x_ref, ng_ref, sh_ref, sc_ref, w_ref, wba_ref, cw_ref, gp_ref,
                 out_ref, ba_ref, h_scr, ext_scr, tail_scr, *, tiles_per_seq, tm, dk):
    i = pl.program_id(0)
    j = pl.program_id(1)

    @pl.when(j == 0)
    def _():
        h = _norm_modulate(x_ref[...], ng_ref[...], sh_ref[...], sc_ref[...])
        hb = h.astype(bf16)
        h_scr[...] = hb
        ba = _dot(hb, wba_ref[...])
        lane = lax.broadcasted_iota(jnp.int32, ba.shape, 1)
        beta = jax.nn.sigmoid(ba)
        g = -jnp.exp(gp_ref[0:1, :]) * _softplus(ba + gp_ref[1:2, :])
        ba_ref[...] = jnp.where(lane < GDN_HEADS, beta, g)

    y = _dot(h_scr[...], w_ref[...])

    @pl.when(j == 3)
    def _():
        out_ref[...] = y.astype(bf16)

    @pl.when(j < 3)
    def _():
        jj = jnp.minimum(j, 2)
        first = (i % tiles_per_seq) == 0

        @pl.when(first)
        def _():
            ext_scr[0:SUBLANES, :] = jnp.zeros((SUBLANES, y.shape[1]), f32)

        @pl.when(jnp.logical_not(first))
        def _():
            ext_scr[0:SUBLANES, :] = tail_scr[jj]

        ext_scr[SUBLANES:, :] = y
        tail_scr[jj] = y[tm - SUBLANES:, :]
        cw = cw_ref[...]
        acc = cw[GDN_CONV - 1:GDN_CONV, :] * y
        for kk in range(GDN_CONV - 1):
            start = SUBLANES - (GDN_CONV - 1) + kk
            acc = acc + cw[kk:kk + 1, :] * ext_scr[pl.ds(start, tm), :]
        a = _silu(acc)

        @pl.when(j == 2)
        def _():
            out_ref[...] = a.astype(bf16)

        @pl.when(j < 2)
        def _():
            post = jnp.where(j == 0, dk ** -0.5, 1.0)
            for h in range(GDN_HEADS):
                sl = slice(h * dk, (h + 1) * dk)
                ah = a[:, sl]
                ss = jnp.sum(ah * ah, axis=-1, keepdims=True)
                out_ref[:, sl] = (ah * (lax.rsqrt(ss + EPS) * post)).astype(bf16)


def _gdn_in_proj(x, mod_l, norm_g, w4, wba, cw, gp, *, seq):
    m, d = x.shape
    tm = ROW_TILE
    tps = seq // tm
    n4 = w4.shape[1]
    tn = n4 // 4
    dk = tn // GDN_HEADS
    body = functools.partial(_gdn_in_body, tiles_per_seq=tps, tm=tm, dk=dk)
    return pl.pallas_call(
        body,
        grid=(m // tm, 4),
        in_specs=[
            pl.BlockSpec((tm, d), lambda i, j: (i, 0)),
            pl.BlockSpec((1, d), lambda i, j: (0, 0)),
            _mod_spec(0, tps, d),
            _mod_spec(1, tps, d),
            pl.BlockSpec((d, tn), lambda i, j: (0, j)),
            pl.BlockSpec((d, LANES), lambda i, j: (0, 0)),
            pl.BlockSpec((GDN_CONV, tn), lambda i, j: (0, j)),
            pl.BlockSpec((SUBLANES, LANES), lambda i, j: (0, 0)),
        ],
        out_specs=[
            pl.BlockSpec((tm, tn), lambda i, j: (i, j)),
            pl.BlockSpec((tm, LANES), lambda i, j: (i, 0)),
        ],
        out_shape=[
            jax.ShapeDtypeStruct((m, n4), bf16),
            jax.ShapeDtypeStruct((m, LANES), f32),
        ],
        scratch_shapes=[
            pltpu.VMEM((tm, d), bf16),
            pltpu.VMEM((tm + SUBLANES, tn), f32),
            pltpu.VMEM((3, SUBLANES, tn), f32),
        ],
        compiler_params=_params("arbitrary", "arbitrary"),
        name="gdn_in_proj",
    )(x, norm_g, mod_l, mod_l, w4, wba, cw, gp)


def _gdn_core_body(q_ref, k_ref, v_ref, z_ref, ba_ref, ng_ref, o_ref,
                   s_scr, u_scr, w_scr, qd_scr, kt_scr, a_scr, gl_scr, *, tb, dk):
    c_len = GDN_CHUNK
    n_chunks = tb // c_len
    heads = GDN_HEADS
    hs = range(heads)

    @pl.when(pl.program_id(1) == 0)
    def _():
        s_scr[...] = jnp.zeros(s_scr.shape, f32)

    row = lax.broadcasted_iota(jnp.int32, (c_len, c_len), 0)
    col = lax.broadcasted_iota(jnp.int32, (c_len, c_len), 1)
    causal = row >= col
    strict = row > col
    tril_f = causal.astype(f32)
    eye_f = (row == col).astype(f32)
    diag_blk = (row // GDN_SOLVE_BLOCK) == (col // GDN_SOLVE_BLOCK)

    def prep(c, carry):
        r0 = pl.multiple_of(c * c_len, c_len)
        rows = pl.ds(r0, c_len)
        ba = ba_ref[rows, :]
        gc_all = jnp.dot(tril_f, ba, precision=lax.Precision.HIGHEST,
                         preferred_element_type=f32)
        gc_t = gc_all.T
        g8 = pl.multiple_of(c * SUBLANES, SUBLANES)
        x, rhs, decay = [], [], []
        for h in hs:
            sl = slice(h * dk, (h + 1) * dk)
            q = q_ref[rows, sl]
            k = k_ref[rows, sl]
            beta = ba[:, h:h + 1]
            gc = gc_all[:, heads + h:heads + h + 1]
            gr = gc_t[heads + h:heads + h + 1, :]
            g_last = gc_all[c_len - 1:c_len, heads + h:heads + h + 1]
            dec = jnp.where(causal, jnp.exp(jnp.minimum(gc - gr, 0.0)), 0.0)
            eg = jnp.exp(gc)
            kf = k.astype(f32)
            kb = kf * beta
            kk = _dot_nt(kb.astype(bf16), k)
            x.append(jnp.where(strict, -(kk * dec), 0.0))
            rhs.append(jnp.concatenate([v_ref[rows, sl].astype(f32) * beta, kb * eg], axis=-1))
            decay.append(dec)
            qd_scr[h, rows, :] = (q.astype(f32) * eg).astype(bf16)
            kt_scr[h, rows, :] = (kf * jnp.exp(g_last - gc)).astype(bf16)
            gl_scr[h, pl.ds(g8, SUBLANES), :] = jnp.broadcast_to(jnp.exp(g_last), (SUBLANES, dk))
        for h in hs:
            sl = slice(h * dk, (h + 1) * dk)
            a_in = jnp.where(causal, _dot_nt(q_ref[rows, sl], k_ref[rows, sl]) * decay[h], 0.0)
            a_scr[h, rows, :] = a_in.astype(bf16)
        x_d = [jnp.where(diag_blk, x[h], 0.0) for h in hs]
        d_inv = [eye_f + x_d[h] for h in hs]
        xp = x_d
        for _ in range(int(math.log2(GDN_SOLVE_BLOCK)) - 1):
            xp = [_dot3(xp[h], xp[h]) for h in hs]
            d_inv = [d_inv[h] + _dot3(d_inv[h], xp[h]) for h in hs]
        mn = [_dot3(d_inv[h], x[h] - x_d[h]) for h in hs]
        o_inv = [eye_f + mn[h] for h in hs]
        mp = mn
        for _ in range(int(math.log2(c_len // GDN_SOLVE_BLOCK)) - 1):
            mp = [_dot3(mp[h], mp[h]) for h in hs]
            o_inv = [o_inv[h] + _dot3(o_inv[h], mp[h]) for h in hs]
        t_inv = [_dot3(o_inv[h], d_inv[h]) for h in hs]
        sol = [_dot3(t_inv[h], rhs[h]) for h in hs]
        for h in hs:
            u_scr[h, rows, :] = sol[h][:, :dk]
            w_scr[h, rows, :] = sol[h][:, dk:].astype(bf16)
        return carry

    lax.fori_loop(0, n_chunks, prep, 0)

    ng = ng_ref[...]

    def scan(c, carry):
        r0 = pl.multiple_of(c * c_len, c_len)
        rows = pl.ds(r0, c_len)
        g8 = pl.multiple_of(c * SUBLANES, SUBLANES)
        s = [s_scr[h] for h in hs]
        sb = [s[h].astype(bf16) for h in hs]
        ws = [_dot(w_scr[h, rows, :], sb[h]) for h in hs]
        qs = [_dot(qd_scr[h, rows, :], sb[h]) for h in hs]
        v_new = [(u_scr[h, rows, :] - ws[h]).astype(bf16) for h in hs]
        o = [qs[h] + _dot(a_scr[h, rows, :], v_new[h]) for h in hs]
        for h in hs:
            gl = gl_scr[h, pl.ds(g8, 1), :]
            s_scr[h] = s[h] * gl + _dot_tn(kt_scr[h, rows, :], v_new[h])
        for h in hs:
            sl = slice(h * dk, (h + 1) * dk)
            z = z_ref[rows, sl].astype(f32)
            ms = jnp.mean(o[h] * o[h], axis=-1, keepdims=True)
            o_ref[rows, sl] = (o[h] * lax.rsqrt(ms + EPS) * ng * _silu(z)).astype(bf16)
        return carry

    lax.fori_loop(0, n_chunks, scan, 0)


def _gdn_core(qkvz, ba, norm_g, *, batch, seq):
    m = qkvz.shape[0]
    d = qkvz.shape[1] // 4
    dk = d // GDN_HEADS
    tb = GDN_TIME_BLOCK
    nt = seq // tb
    n_chunks = tb // GDN_CHUNK

    def col(jc):
        return pl.BlockSpec((tb, d), lambda b, t: (b * nt + t, jc))

    body = functools.partial(_gdn_core_body, tb=tb, dk=dk)
    return pl.pallas_call(
        body,
        grid=(batch, nt),
        in_specs=[col(0), col(1), col(2), col(3),
                  pl.BlockSpec((tb, LANES), lambda b, t: (b * nt + t, 0)),
                  pl.BlockSpec((1, dk), lambda b, t: (0, 0))],
        out_specs=pl.BlockSpec((tb, d), lambda b, t: (b * nt + t, 0)),
        out_shape=jax.ShapeDtypeStruct((m, d), bf16),
        scratch_shapes=[
            pltpu.VMEM((GDN_HEADS, dk, dk), f32),
            pltpu.VMEM((GDN_HEADS, tb, dk), f32),
            pltpu.VMEM((GDN_HEADS, tb, dk), bf16),
            pltpu.VMEM((GDN_HEADS, tb, dk), bf16),
            pltpu.VMEM((GDN_HEADS, tb, dk), bf16),
            pltpu.VMEM((GDN_HEADS, tb, GDN_CHUNK), bf16),
            pltpu.VMEM((GDN_HEADS, n_chunks * SUBLANES, dk), f32),
        ],
        compiler_params=_params("arbitrary", "arbitrary"),
        name="gdn_core",
    )(qkvz, qkvz, qkvz, qkvz, ba, norm_g)


def _out_proj_body(x_ref, o_ref, w_ref, g_ref, out_ref):
    out_ref[...] = x_ref[...] + g_ref[...] * _dot(o_ref[...], w_ref[...])


def _out_proj(x, o, w, mod_l, gate_idx, *, seq):
    m, d = x.shape
    tm = ROW_TILE
    tps = seq // tm
    return pl.pallas_call(
        _out_proj_body,
        grid=(m // tm,),
        in_specs=[
            pl.BlockSpec((tm, d), lambda i: (i, 0)),
            pl.BlockSpec((tm, d), lambda i: (i, 0)),
            pl.BlockSpec((d, d), lambda i: (0, 0)),
            _mod_spec(gate_idx, tps, d),
        ],
        out_specs=pl.BlockSpec((tm, d), lambda i: (i, 0)),
        out_shape=jax.ShapeDtypeStruct((m, d), f32),
        compiler_params=_params("arbitrary"),
        name="out_proj",
    )(x, o, w, mod_l)


def _dsw_in_body(x_ref, ng_ref, sh_ref, sc_ref, w_ref, gn_ref, cos_ref, sin_ref,
                 o0_ref, o1_ref, o2_ref, h_scr, slab_scr, *, dh, tm):
    j = pl.program_id(1)

    @pl.when(j == 0)
    def _():
        h = _norm_modulate(x_ref[...], ng_ref[...], sh_ref[...], sc_ref[...])
        h_scr[...] = h.astype(bf16)

    y = _dot(h_scr[...], w_ref[...])
    which = j % 3
    gi = j // 3
    out_refs = (o0_ref, o1_ref, o2_ref)

    def write_head(g, h, val):
        dil = DSW_PATTERNS[g][1]
        n = tm // dil
        sl = slice(h * dh, (h + 1) * dh)
        if dil == 1:
            out_refs[g][0, :, sl] = val.astype(bf16)
        else:
            slab_scr[h] = val
            for r in range(dil):
                out_refs[g][r, :, sl] = slab_scr[h, pl.ds(r, n, stride=dil), :].astype(bf16)

    for g in range(len(DSW_PATTERNS)):
        @pl.when((gi == g) & (which == 2))
        def _(g=g):
            for h in range(DSW_HEADS):
                write_head(g, h, y[:, h * dh:(h + 1) * dh])

        @pl.when((gi == g) & (which < 2))
        def _(g=g):
            cos_t = cos_ref[...]
            sin_t = sin_ref[...]
            for h in range(DSW_HEADS):
                sl = slice(h * dh, (h + 1) * dh)
                yh = y[:, sl]
                ms = jnp.mean(yh * yh, axis=-1, keepdims=True)
                yn = yh * lax.rsqrt(ms + EPS) * gn_ref[:, sl]
                write_head(g, h, yn * cos_t + pltpu.roll(yn, dh // 2, 1) * sin_t)


def _dsw_in_proj(x, mod_l, norm_g, w, gn, cos_t, sin_t, *, batch, seq):
    m, d = x.shape
    tm = ROW_TILE
    tps = seq // tm
    n = w.shape[1]
    tn = d
    dh = d // DSW_HEADS
    body = functools.partial(_dsw_in_body, dh=dh, tm=tm)
    rope_spec = pl.BlockSpec((tm, dh), lambda i, j: (i % tps, 0))

    def out_spec(g):
        dil = DSW_PATTERNS[g][1]
        return pl.BlockSpec((None, dil, tm // dil, tn),
                            lambda i, j: (i // tps, 0, i % tps, jnp.clip(j - 3 * g, 0, 2)))

    groups = range(len(DSW_PATTERNS))
    return pl.pallas_call(
        body,
        grid=(m // tm, n // tn),
        in_specs=[
            pl.BlockSpec((tm, d), lambda i, j: (i, 0)),
            pl.BlockSpec((1, d), lambda i, j: (0, 0)),
            _mod_spec(0, tps, d),
            _mod_spec(1, tps, d),
            pl.BlockSpec((d, tn), lambda i, j: (0, j)),
            pl.BlockSpec((None, 1, tn), lambda i, j: (j, 0, 0)),
            rope_spec, rope_spec,
        ],
        out_specs=[out_spec(g) for g in groups],
        out_shape=[jax.ShapeDtypeStruct((batch, DSW_PATTERNS[g][1], seq // DSW_PATTERNS[g][1], 3 * d), bf16)
                   for g in groups],
        scratch_shapes=[pltpu.VMEM((tm, d), bf16), pltpu.VMEM((DSW_HEADS, tm, dh), f32)],
        compiler_params=_params("arbitrary", "arbitrary"),
        name="dsw_in_proj",
    )(x, norm_g, mod_l, mod_l, w, gn, cos_t, sin_t)


def _dsw_attn_body(q_ref, kp_ref, kc_ref, vp_ref, vc_ref, o_ref, lse_ref, *, dh, span):
    blk = pl.program_id(2)
    p_len = DSW_BLOCK
    qi = lax.broadcasted_iota(jnp.int32, (p_len, 2 * p_len), 0)
    ki = lax.broadcasted_iota(jnp.int32, (p_len, 2 * p_len), 1)
    rel = qi + p_len - ki
    valid = (rel >= 0) & (rel <= span) & ((blk > 0) | (ki >= p_len))
    lane = lax.broadcasted_iota(jnp.int32, (p_len, LANES), 1)
    lse_tile = jnp.zeros((p_len, LANES), f32)
    hs = range(DSW_HEADS)
    scores = []
    for h in hs:
        sl = slice(h * dh, (h + 1) * dh)
        kcat = jnp.concatenate([kp_ref[:, sl], kc_ref[:, sl]], axis=0)
        scores.append(_dot_nt(q_ref[:, sl], kcat))
    for h in hs:
        sl = slice(h * dh, (h + 1) * dh)
        vcat = jnp.concatenate([vp_ref[:, sl], vc_ref[:, sl]], axis=0)
        s = jnp.where(valid, scores[h], -jnp.inf)
        mx = jnp.max(s, axis=-1, keepdims=True)
        p = jnp.exp(s - mx)
        l = jnp.sum(p, axis=-1, keepdims=True)
        o = _dot(p.astype(bf16), vcat) * (1.0 / l)
        o_ref[:, sl] = o.astype(bf16)
        lse_tile = jnp.where(lane == h, mx + jnp.log(l), lse_tile)
    lse_ref[...] = lse_tile


def _dsw_attention(qkv, window, dilation, *, d):
    batch, _, ts, _ = qkv.shape
    dh = d // DSW_HEADS
    nb = ts // DSW_BLOCK
    span = window // dilation

    def in_spec(which, prev):
        def imap(b, r, blk):
            return (b, r, jnp.maximum(blk - 1, 0) if prev else blk, which)
        return pl.BlockSpec((None, None, DSW_BLOCK, d), imap)

    body = functools.partial(_dsw_attn_body, dh=dh, span=span)
    return pl.pallas_call(
        body,
        grid=(batch, dilation, nb),
        in_specs=[in_spec(0, False), in_spec(1, True), in_spec(1, False),
                  in_spec(2, True), in_spec(2, False)],
        out_specs=[
            pl.BlockSpec((None, None, DSW_BLOCK, d), lambda b, r, blk: (b, r, blk, 0)),
            pl.BlockSpec((None, None, DSW_BLOCK, LANES), lambda b, r, blk: (b, r, blk, 0)),
        ],
        out_shape=[
            jax.ShapeDtypeStruct((batch, dilation, ts, d), bf16),
            jax.ShapeDtypeStruct((batch, dilation, ts, LANES), f32),
        ],
        compiler_params=_params("arbitrary", "arbitrary", "arbitrary"),
        name=f"dsw_attn_d{dilation}",
    )(qkv, qkv, qkv, qkv, qkv)


def _dsw_out_body(x_ref, o0_ref, o1_ref, o2_ref, l0_ref, l1_ref, l2_ref, w_ref, g_ref,
                  out_ref, lse_scr, nat_scr, om_scr, *, dh, tm):
    o_refs = (o0_ref, o1_ref, o2_ref)
    l_refs = (l0_ref, l1_ref, l2_ref)
    groups = range(len(DSW_PATTERNS))

    def token_rows(g, r):
        dil = DSW_PATTERNS[g][1]
        return pl.ds(r, tm // dil, stride=dil)

    for g in groups:
        for r in range(DSW_PATTERNS[g][1]):
            lse_scr[g, token_rows(g, r), :] = l_refs[g][r]
    lse = [lse_scr[g] for g in groups]
    mx = jnp.maximum(jnp.maximum(lse[0], lse[1]), lse[2])
    e = [jnp.exp(lse[g] - mx) for g in groups]
    inv = 1.0 / (e[0] + e[1] + e[2])
    wts = [e[g] * inv for g in groups]
    for h in range(DSW_HEADS):
        sl = slice(h * dh, (h + 1) * dh)
        om = None
        for g in groups:
            for r in range(DSW_PATTERNS[g][1]):
                nat_scr[g, token_rows(g, r), :] = o_refs[g][r, :, sl].astype(f32)
            term = wts[g][:, h:h + 1] * nat_scr[g]
            om = term if om is None else om + term
        om_scr[:, sl] = om.astype(bf16)
    out_ref[...] = x_ref[...] + g_ref[...] * _dot(om_scr[...], w_ref[...])


def _dsw_out_proj(x, outs, lses, w, mod_l, *, seq):
    m, d = x.shape
    tm = DSW_OUT_ROW_TILE
    tps = seq // tm
    dh = d // DSW_HEADS
    row = pl.BlockSpec((tm, d), lambda i: (i, 0))

    def stream_spec(g, width):
        dil = DSW_PATTERNS[g][1]
        return pl.BlockSpec((None, dil, tm // dil, width), lambda i: (i // tps, 0, i % tps, 0))

    groups = range(len(DSW_PATTERNS))
    body = functools.partial(_dsw_out_body, dh=dh, tm=tm)
    return pl.pallas_call(
        body,
        grid=(m // tm,),
        in_specs=[row] + [stream_spec(g, d) for g in groups] + [stream_spec(g, LANES) for g in groups]
                 + [pl.BlockSpec((d, d), lambda i: (0, 0)), _mod_spec(2, tps, d)],
        out_specs=row,
        out_shape=jax.ShapeDtypeStruct((m, d), f32),
        scratch_shapes=[pltpu.VMEM((len(DSW_PATTERNS), tm, LANES), f32),
                        pltpu.VMEM((len(DSW_PATTERNS), tm, dh), f32),
                        pltpu.VMEM((tm, d), bf16)],
        compiler_params=_params("arbitrary"),
        name="dsw_out_proj",
    )(x, *outs, *lses, w, mod_l)


def _ffn_body(x_ref, ng_ref, sh_ref, sc_ref, g_ref, wg_ref, wu_ref, wd_ref, out_ref,
              h_scr, acc_scr):
    j = pl.program_id(1)

    @pl.when(j == 0)
    def _():
        h = _norm_modulate(x_ref[...], ng_ref[...], sh_ref[...], sc_ref[...])
        h_scr[...] = h.astype(bf16)
        acc_scr[...] = jnp.zeros(acc_scr.shape, f32)

    hb = h_scr[...]
    gate = _dot(hb, wg_ref[...])
    up = _dot(hb, wu_ref[...])
    act = (_silu(gate) * up).astype(bf16)
    acc_scr[...] += _dot(act, wd_ref[...])

    @pl.when(j == pl.num_programs(1) - 1)
    def _():
        out_ref[...] = x_ref[...] + g_ref[...] * acc_scr[...]


def _ffn(x, mod_l, norm_g, w_gate_up, w_down, *, seq):
    m, d = x.shape
    tm = ROW_TILE
    tps = seq // tm
    hidden = w_down.shape[0]
    th = 256
    nh = hidden // th
    return pl.pallas_call(
        _ffn_body,
        grid=(m // tm, nh),
        in_specs=[
            pl.BlockSpec((tm, d), lambda i, j: (i, 0)),
            pl.BlockSpec((1, d), lambda i, j: (0, 0)),
            _mod_spec(3, tps, d),
            _mod_spec(4, tps, d),
            _mod_spec(5, tps, d),
            pl.BlockSpec((d, th), lambda i, j: (0, j)),
            pl.BlockSpec((d, th), lambda i, j: (0, j + nh)),
            pl.BlockSpec((th, d), lambda i, j: (j, 0)),
        ],
        out_specs=pl.BlockSpec((tm, d), lambda i, j: (i, 0)),
        out_shape=jax.ShapeDtypeStruct((m, d), f32),
        scratch_shapes=[pltpu.VMEM((tm, d), bf16), pltpu.VMEM((tm, d), f32)],
        compiler_params=_params("arbitrary", "arbitrary"),
        name="ffn",
    )(x, norm_g, mod_l, mod_l, mod_l, w_gate_up, w_gate_up, w_down)


def _rope_tables(seq, dh):
    rope_dim = dh // 4
    half = rope_dim // 2
    inv = jnp.exp(-math.log(ROPE_THETA) * (2.0 * jnp.arange(half, dtype=f32) / rope_dim))
    ang = jnp.arange(seq, dtype=f32)[:, None] * inv[None, :]
    cos = jnp.cos(ang)
    sin = jnp.sin(ang)
    ones = jnp.ones((seq, dh // 2 - half), f32)
    zeros = jnp.zeros((seq, dh // 2 - half), f32)
    cos_t = jnp.concatenate([cos, ones, cos, ones], axis=-1)
    sin_t = jnp.concatenate([-sin, zeros, sin, zeros], axis=-1)
    return cos_t, sin_t


def _rope_lane_order(t, dh):
    half = (dh // 4) // 2
    return jnp.concatenate([t[..., :half], t[..., 2 * half:dh // 2 + half],
                            t[..., half:2 * half], t[..., dh // 2 + half:]], axis=-1)


def kernel(x, c, mod_w, mod_b, mix_norm_g, ffn_norm_g, gdn_w_in, gdn_conv_w, gdn_A_log,
           gdn_dt_bias, gdn_norm_g, gdn_w_out, dsw_w_in, dsw_q_norm_g, dsw_k_norm_g,
           dsw_w_out, ffn_w_gate_up, ffn_w_down):
    batch, seq, d = x.shape
    depth = mod_w.shape[0]
    m = batch * seq
    heads = GDN_HEADS
    dh = d // DSW_HEADS
    assert seq % ROW_TILE == 0 and seq % GDN_TIME_BLOCK == 0 and d == heads * LANES

    mod_rows = 16
    c_pad = jnp.zeros((mod_rows, d), f32).at[:batch].set(c)
    mod = _modulation(c_pad, mod_w, mod_b)[:, :batch]
    mod = mod.reshape(depth, batch, 6, 1, d)

    cos_t, sin_t = _rope_tables(seq, dh)
    xf = x.reshape(m, d)

    for layer in range(depth):
        mod_l = mod[layer]
        j = layer // N_MIXERS
        norm_g = mix_norm_g[layer].reshape(1, d)
        if layer % N_MIXERS == 0:
            n_conv = 3 * d
            w_in = gdn_w_in[j]
            w4 = w_in[:, :4 * d].astype(bf16)
            wba = jnp.zeros((d, LANES), f32).at[:, :2 * heads].set(w_in[:, 4 * d:]).astype(bf16)
            cw = jnp.zeros((GDN_CONV, 4 * d), f32).at[:, :n_conv].set(gdn_conv_w[j])
            gp = jnp.zeros((SUBLANES, LANES), f32)
            gp = gp.at[0, heads:2 * heads].set(gdn_A_log[j]).at[1, heads:2 * heads].set(gdn_dt_bias[j])
            qkvz, ba = _gdn_in_proj(xf, mod_l, norm_g, w4, wba, cw, gp, seq=seq)
            o = _gdn_core(qkvz, ba, gdn_norm_g[j].reshape(1, -1), batch=batch, seq=seq)
            xf = _out_proj(xf, o, gdn_w_out[j].astype(bf16), mod_l, 2, seq=seq)
        else:
            n_groups = len(DSW_PATTERNS)
            w5 = dsw_w_in[j].reshape(d, n_groups, 3, DSW_HEADS, dh)
            w5 = jnp.concatenate([_rope_lane_order(w5[:, :, :2], dh), w5[:, :, 2:]], axis=2)
            w_in = w5.reshape(d, -1).astype(bf16)
            gains = []
            for gi in range(n_groups):
                q_gain = _rope_lane_order(dsw_q_norm_g[j, gi], dh) * (dh ** -0.5)
                gains.append(jnp.tile(q_gain, DSW_HEADS))
                gains.append(jnp.tile(_rope_lane_order(dsw_k_norm_g[j, gi], dh), DSW_HEADS))
                gains.append(jnp.ones((d,), f32))
            gn = jnp.stack(gains).reshape(3 * n_groups, 1, d)
            qkvs = _dsw_in_proj(xf, mod_l, norm_g, w_in, gn, cos_t, sin_t, batch=batch, seq=seq)
            outs, lses = [], []
            for gi, (window, dilation) in enumerate(DSW_PATTERNS):
                o_g, lse_g = _dsw_attention(qkvs[gi], window, dilation, d=d)
                outs.append(o_g)
                lses.append(lse_g)
            xf = _dsw_out_proj(xf, outs, lses, dsw_w_out[j].astype(bf16), mod_l, seq=seq)
        xf = _ffn(xf, mod_l, ffn_norm_g[layer].reshape(1, d), ffn_w_gate_up[layer].astype(bf16),
                  ffn_w_down[layer].astype(bf16), seq=seq)
    return xf.reshape(batch, seq, d)
```

```python
import functools
import math

import jax
import jax.numpy as jnp
from jax import lax
from jax.experimental import pallas as pl
from jax.experimental.pallas import tpu as pltpu

EPS = 1e-6
N_MIXERS = 2
GDN_HEADS = 8
GDN_CONV = 4
GDN_CHUNK = 64
GDN_SOLVE_BLOCK = 16
DSW_PATTERNS = ((128, 1), (512, 4), (2048, 16))
DSW_HEADS = 8
DSW_BLOCK = 128
ROPE_THETA = 500000.0

LANES = 128
SUBLANES = 8
VMEM_LIMIT_BYTES = 56 * 1024 * 1024

ROW_TILE = 1024
DSW_OUT_ROW_TILE = 512
GDN_TIME_BLOCK = 512
FFN_HIDDEN_TILE = 256

f32 = jnp.float32
bf16 = jnp.bfloat16


def _params(*sem):
    return pltpu.CompilerParams(dimension_semantics=sem, vmem_limit_bytes=VMEM_LIMIT_BYTES)


def _dot(a, b):
    return jnp.dot(a, b, preferred_element_type=f32)


def _dot_nt(a, b):
    return lax.dot_general(a, b, (((1,), (1,)), ((), ())), preferred_element_type=f32)


def _dot_tn(a, b):
    return lax.dot_general(a, b, (((0,), (0,)), ((), ())), preferred_element_type=f32)


def _dot_round(a, b):
    return _dot(a.astype(bf16), b.astype(bf16))


def _silu(x):
    return x * (0.5 + 0.5 * jnp.tanh(0.5 * x))


def _softplus(x):
    return jnp.maximum(x, 0.0) + jnp.log1p(jnp.exp(-jnp.abs(x)))


def _norm_modulate(x, gain, shift, scale):
    ms = jnp.mean(x * x, axis=-1, keepdims=True)
    y = x * lax.rsqrt(ms + EPS) * gain
    return y * (1.0 + scale) + shift


def _mod_body(c_ref, w_ref, b_ref, o_ref):
    c = c_ref[...]
    cond = _silu(c).astype(bf16)
    o_ref[...] = _dot(cond, w_ref[...].astype(bf16)) + b_ref[...]


def _modulation(c_pad, mod_w, mod_b):
    depth, d, n = mod_w.shape
    rows = c_pad.shape[0]
    tn = 1536
    return pl.pallas_call(
        _mod_body,
        grid=(depth, n // tn),
        in_specs=[
            pl.BlockSpec((rows, d), lambda l, j: (0, 0)),
            pl.BlockSpec((None, d, tn), lambda l, j: (l, 0, j)),
            pl.BlockSpec((None, 1, tn), lambda l, j: (l, 0, j)),
        ],
        out_specs=pl.BlockSpec((None, rows, tn), lambda l, j: (l, 0, j)),
        out_shape=jax.ShapeDtypeStruct((depth, rows, n), f32),
        compiler_params=_params("arbitrary", "arbitrary"),
        name="modulation",
    )(c_pad, mod_w, mod_b.reshape(depth, 1, n))


def _mod_spec(idx, tiles_per_seq, d):
    return pl.BlockSpec((None, None, 1, d), lambda i, *_: (i // tiles_per_seq, idx, 0, 0))


def _gdn_in_body(x_ref, ng_ref, sh_ref, sc_ref, w_ref, wba_ref, cw_ref, gp_ref,
                 out_ref, ba_ref, h_scr, ya_scr, yb_scr, tail_scr, *, tiles_per_seq, tm, dk, n_tiles):
    i = pl.program_id(0)
    j = pl.program_id(1)
    y_scr = (ya_scr, yb_scr)
    first = (i % tiles_per_seq) == 0
    n_conv_tiles = n_tiles - 1

    @pl.when((i == 0) & (j == 0))
    def _():
        tail_scr[...] = jnp.zeros(tail_scr.shape, f32)

    def epilogue(tile):
        y_ref = y_scr[tile % 2]
        if tile == n_conv_tiles:
            out_ref[...] = y_ref[SUBLANES:, :].astype(bf16)
            return
        y_ref[0:SUBLANES, :] = jnp.where(first, 0.0, tail_scr[tile])
        tail_scr[tile] = y_ref[tm:, :]
        post = dk ** -0.5 if tile == 0 else 1.0
        for h in range(GDN_HEADS):
            sl = slice(h * dk, (h + 1) * dk)
            acc = cw_ref[GDN_CONV - 1:GDN_CONV, sl] * y_ref[SUBLANES:, sl]
            for kk in range(GDN_CONV - 1):
                start = SUBLANES - (GDN_CONV - 1) + kk
                acc = acc + cw_ref[kk:kk + 1, sl] * y_ref[pl.ds(start, tm), sl]
            a = _silu(acc)
            if tile < 2:
                ss = jnp.sum(a * a, axis=-1, keepdims=True)
                a = a * (lax.rsqrt(ss + EPS) * post)
            out_ref[:, sl] = a.astype(bf16)

    for step in range(n_tiles + 1):
        @pl.when(j == step)
        def _(step=step):
            if step == 0:
                h = _norm_modulate(x_ref[...], ng_ref[...], sh_ref[...], sc_ref[...])
                hb = h.astype(bf16)
                h_scr[...] = hb
                ba = _dot(hb, wba_ref[...])
                lane = lax.broadcasted_iota(jnp.int32, ba.shape, 1)
                beta = jax.nn.sigmoid(ba)
                g = -jnp.exp(gp_ref[0:1, :]) * _softplus(ba + gp_ref[1:2, :])
                ba_ref[...] = jnp.where(lane < GDN_HEADS, beta, g)
            if step < n_tiles:
                y_scr[step % 2][SUBLANES:, :] = _dot(h_scr[...], w_ref[...])
            if step > 0:
                epilogue(step - 1)


def _gdn_in_proj(x, mod_l, norm_g, w4, wba, cw, gp, *, seq):
    m, d = x.shape
    tm = ROW_TILE
    tps = seq // tm
    n_tiles = 4
    n4 = w4.shape[1]
    tn = n4 // n_tiles
    dk = tn // GDN_HEADS
    body = functools.partial(_gdn_in_body, tiles_per_seq=tps, tm=tm, dk=dk, n_tiles=n_tiles)

    def done(j):
        return jnp.clip(j - 1, 0, n_tiles - 1)

    return pl.pallas_call(
        body,
        grid=(m // tm, n_tiles + 1),
        in_specs=[
            pl.BlockSpec((tm, d), lambda i, j: (i, 0)),
            pl.BlockSpec((1, d), lambda i, j: (0, 0)),
            _mod_spec(0, tps, d),
            _mod_spec(1, tps, d),
            pl.BlockSpec((d, tn), lambda i, j: (0, jnp.minimum(j, n_tiles - 1))),
            pl.BlockSpec((d, LANES), lambda i, j: (0, 0)),
            pl.BlockSpec((GDN_CONV, tn), lambda i, j: (0, done(j))),
            pl.BlockSpec((SUBLANES, LANES), lambda i, j: (0, 0)),
        ],
        out_specs=[
            pl.BlockSpec((tm, tn), lambda i, j: (i, done(j))),
            pl.BlockSpec((tm, LANES), lambda i, j: (i, 0)),
        ],
        out_shape=[
            jax.ShapeDtypeStruct((m, n4), bf16),
            jax.ShapeDtypeStruct((m, LANES), f32),
        ],
        scratch_shapes=[
            pltpu.VMEM((tm, d), bf16),
            pltpu.VMEM((tm + SUBLANES, tn), f32),
            pltpu.VMEM((tm + SUBLANES, tn), f32),
            pltpu.VMEM((n_tiles - 1, SUBLANES, tn), f32),
        ],
        compiler_params=_params("arbitrary", "arbitrary"),
        name="gdn_in_proj",
    )(x, norm_g, mod_l, mod_l, w4, wba, cw, gp)


def _gdn_core_body(q_ref, k_ref, v_ref, z_ref, ba_ref, ng_ref, o_ref,
                   s_scr, u_scr, w_scr, qd_scr, kt_scr, a_scr, gl_scr, *, tb, dk):
    c_len = GDN_CHUNK
    n_chunks = tb // c_len
    heads = GDN_HEADS
    hs = range(heads)

    @pl.when(pl.program_id(1) == 0)
    def _():
        s_scr[...] = jnp.zeros(s_scr.shape, f32)

    row = lax.broadcasted_iota(jnp.int32, (c_len, c_len), 0)
    col = lax.broadcasted_iota(jnp.int32, (c_len, c_len), 1)
    causal = row >= col
    strict = row > col
    tril_f = causal.astype(f32)
    eye_f = (row == col).astype(f32)
    diag_blk = (row // GDN_SOLVE_BLOCK) == (col // GDN_SOLVE_BLOCK)

    def prep(c, carry):
        r0 = pl.multiple_of(c * c_len, c_len)
        rows = pl.ds(r0, c_len)
        ba = ba_ref[rows, :]
        gc_all = jnp.dot(tril_f, ba, precision=lax.Precision.HIGHEST,
                         preferred_element_type=f32)
        gc_t = gc_all.T
        g8 = pl.multiple_of(c * SUBLANES, SUBLANES)
        x, rhs, decay = [], [], []
        for h in hs:
            sl = slice(h * dk, (h + 1) * dk)
            q = q_ref[rows, sl]
            k = k_ref[rows, sl]
            beta = ba[:, h:h + 1]
            gc = gc_all[:, heads + h:heads + h + 1]
            gr = gc_t[heads + h:heads + h + 1, :]
            g_last = gc_all[c_len - 1:c_len, heads + h:heads + h + 1]
            dec = jnp.where(causal, jnp.exp(jnp.minimum(gc - gr, 0.0)), 0.0)
            eg = jnp.exp(gc)
            kf = k.astype(f32)
            kb = kf * beta
            kk = _dot_nt(kb.astype(bf16), k)
            x.append(jnp.where(strict, -(kk * dec), 0.0))
            rhs.append(jnp.concatenate([v_ref[rows, sl].astype(f32) * beta, kb * eg], axis=-1))
            decay.append(dec)
            qd_scr[h, rows, :] = (q.astype(f32) * eg).astype(bf16)
            kt_scr[h, rows, :] = (kf * jnp.exp(g_last - gc)).astype(bf16)
            gl_scr[h, pl.ds(g8, SUBLANES), :] = jnp.broadcast_to(jnp.exp(g_last), (SUBLANES, dk))
        for h in hs:
            sl = slice(h * dk, (h + 1) * dk)
            a_in = jnp.where(causal, _dot_nt(q_ref[rows, sl], k_ref[rows, sl]) * decay[h], 0.0)
            a_scr[h, rows, :] = a_in.astype(bf16)
        x_d = [jnp.where(diag_blk, x[h], 0.0) for h in hs]
        d_inv = [eye_f + x_d[h] for h in hs]
        xp = x_d
        for _ in range(int(math.log2(GDN_SOLVE_BLOCK)) - 1):
            xp = [_dot_round(xp[h], xp[h]) for h in hs]
            d_inv = [d_inv[h] + _dot_round(d_inv[h], xp[h]) for h in hs]
        mn = [_dot_round(d_inv[h], x[h] - x_d[h]) for h in hs]
        o_inv = [eye_f + mn[h] for h in hs]
        mp = mn
        for _ in range(int(math.log2(c_len // GDN_SOLVE_BLOCK)) - 1):
            mp = [_dot_round(mp[h], mp[h]) for h in hs]
            o_inv = [o_inv[h] + _dot_round(o_inv[h], mp[h]) for h in hs]
        t_inv = [_dot_round(o_inv[h], d_inv[h]) for h in hs]
        sol = [_dot_round(t_inv[h], rhs[h]) for h in hs]
        for h in hs:
            u_scr[h, rows, :] = sol[h][:, :dk]
            w_scr[h, rows, :] = sol[h][:, dk:].astype(bf16)
        return carry

    lax.fori_loop(0, n_chunks, prep, 0)

    ng = ng_ref[...]

    def scan(c, carry):
        r0 = pl.multiple_of(c * c_len, c_len)
        rows = pl.ds(r0, c_len)
        g8 = pl.multiple_of(c * SUBLANES, SUBLANES)
        s = [s_scr[h] for h in hs]
        sb = [s[h].astype(bf16) for h in hs]
        ws = [_dot(w_scr[h, rows, :], sb[h]) for h in hs]
        qs = [_dot(qd_scr[h, rows, :], sb[h]) for h in hs]
        v_new = [(u_scr[h, rows, :] - ws[h]).astype(bf16) for h in hs]
        o = [qs[h] + _dot(a_scr[h, rows, :], v_new[h]) for h in hs]
        for h in hs:
            gl = gl_scr[h, pl.ds(g8, 1), :]
            s_scr[h] = s[h] * gl + _dot_tn(kt_scr[h, rows, :], v_new[h])
        for h in hs:
            sl = slice(h * dk, (h + 1) * dk)
            z = z_ref[rows, sl].astype(f32)
            ms = jnp.mean(o[h] * o[h], axis=-1, keepdims=True)
            o_ref[rows, sl] = (o[h] * lax.rsqrt(ms + EPS) * ng * _silu(z)).astype(bf16)
        return carry

    lax.fori_loop(0, n_chunks, scan, 0)


def _gdn_core(qkvz, ba, norm_g, *, batch, seq):
    m = qkvz.shape[0]
    d = qkvz.shape[1] // 4
    dk = d // GDN_HEADS
    tb = GDN_TIME_BLOCK
    nt = seq // tb
    n_chunks = tb // GDN_CHUNK

    def col(jc):
        return pl.BlockSpec((tb, d), lambda b, t: (b * nt + t, jc))

    body = functools.partial(_gdn_core_body, tb=tb, dk=dk)
    return pl.pallas_call(
        body,
        grid=(batch, nt),
        in_specs=[col(0), col(1), col(2), col(3),
                  pl.BlockSpec((tb, LANES), lambda b, t: (b * nt + t, 0)),
                  pl.BlockSpec((1, dk), lambda b, t: (0, 0))],
        out_specs=pl.BlockSpec((tb, d), lambda b, t: (b * nt + t, 0)),
        out_shape=jax.ShapeDtypeStruct((m, d), bf16),
        scratch_shapes=[
            pltpu.VMEM((GDN_HEADS, dk, dk), f32),
            pltpu.VMEM((GDN_HEADS, tb, dk), f32),
            pltpu.VMEM((GDN_HEADS, tb, dk), bf16),
            pltpu.VMEM((GDN_HEADS, tb, dk), bf16),
            pltpu.VMEM((GDN_HEADS, tb, dk), bf16),
            pltpu.VMEM((GDN_HEADS, tb, GDN_CHUNK), bf16),
            pltpu.VMEM((GDN_HEADS, n_chunks * SUBLANES, dk), f32),
        ],
        compiler_params=_params("arbitrary", "arbitrary"),
        name="gdn_core",
    )(qkvz, qkvz, qkvz, qkvz, ba, norm_g)


def _out_proj_body(x_ref, o_ref, w_ref, g_ref, out_ref):
    out_ref[...] = x_ref[...] + g_ref[...] * _dot(o_ref[...], w_ref[...])


def _out_proj(x, o, w, mod_l, gate_idx, *, seq):
    m, d = x.shape
    tm = ROW_TILE
    tps = seq // tm
    return pl.pallas_call(
        _out_proj_body,
        grid=(m // tm,),
        in_specs=[
            pl.BlockSpec((tm, d), lambda i: (i, 0)),
            pl.BlockSpec((tm, d), lambda i: (i, 0)),
            pl.BlockSpec((d, d), lambda i: (0, 0)),
            _mod_spec(gate_idx, tps, d),
        ],
        out_specs=pl.BlockSpec((tm, d), lambda i: (i, 0)),
        out_shape=jax.ShapeDtypeStruct((m, d), f32),
        compiler_params=_params("arbitrary"),
        name="out_proj",
    )(x, o, w, mod_l)


def _dsw_in_body(x_ref, ng_ref, sh_ref, sc_ref, w_ref, gn_ref, cos_ref, sin_ref,
                 o0_ref, o1_ref, o2_ref, h_scr, ya_scr, yb_scr, slab_scr, *, dh, tm, n_tiles):
    j = pl.program_id(1)
    y_scr = (ya_scr, yb_scr)
    out_refs = (o0_ref, o1_ref, o2_ref)

    def write_head(g, h, val):
        dil = DSW_PATTERNS[g][1]
        n = tm // dil
        sl = slice(h * dh, (h + 1) * dh)
        if dil == 1:
            out_refs[g][0, :, sl] = val.astype(bf16)
        else:
            slab_scr[h] = val
            for r in range(dil):
                out_refs[g][r, :, sl] = slab_scr[h, pl.ds(r, n, stride=dil), :].astype(bf16)

    def epilogue(tile):
        g, which = divmod(tile, 3)
        y_ref = y_scr[tile % 2]
        if which == 2:
            for h in range(DSW_HEADS):
                write_head(g, h, y_ref[:, h * dh:(h + 1) * dh])
            return
        cos_t = cos_ref[...]
        sin_t = sin_ref[...]
        for h in range(DSW_HEADS):
            sl = slice(h * dh, (h + 1) * dh)
            yh = y_ref[:, sl]
            ms = jnp.mean(yh * yh, axis=-1, keepdims=True)
            yn = yh * lax.rsqrt(ms + EPS) * gn_ref[:, sl]
            write_head(g, h, yn * cos_t + pltpu.roll(yn, dh // 2, 1) * sin_t)

    for step in range(n_tiles + 1):
        @pl.when(j == step)
        def _(step=step):
            if step == 0:
                h = _norm_modulate(x_ref[...], ng_ref[...], sh_ref[...], sc_ref[...])
                h_scr[...] = h.astype(bf16)
            if step < n_tiles:
                y_scr[step % 2][...] = _dot(h_scr[...], w_ref[...])
            if step > 0:
                epilogue(step - 1)


def _dsw_in_proj(x, mod_l, norm_g, w, gn, cos_t, sin_t, *, batch, seq):
    m, d = x.shape
    tm = ROW_TILE
    tps = seq // tm
    n = w.shape[1]
    tn = d
    dh = d // DSW_HEADS
    n_tiles = n // tn
    body = functools.partial(_dsw_in_body, dh=dh, tm=tm, n_tiles=n_tiles)
    rope_spec = pl.BlockSpec((tm, dh), lambda i, j: (i % tps, 0))

    def out_spec(g):
        dil = DSW_PATTERNS[g][1]
        return pl.BlockSpec((None, dil, tm // dil, tn),
                            lambda i, j: (i // tps, 0, i % tps, jnp.clip(j - 1 - 3 * g, 0, 2)))

    groups = range(len(DSW_PATTERNS))
    return pl.pallas_call(
        body,
        grid=(m // tm, n_tiles + 1),
        in_specs=[
            pl.BlockSpec((tm, d), lambda i, j: (i, 0)),
            pl.BlockSpec((1, d), lambda i, j: (0, 0)),
            _mod_spec(0, tps, d),
            _mod_spec(1, tps, d),
            pl.BlockSpec((d, tn), lambda i, j: (0, jnp.minimum(j, n_tiles - 1))),
            pl.BlockSpec((None, 1, tn), lambda i, j: (jnp.maximum(j - 1, 0), 0, 0)),
            rope_spec, rope_spec,
        ],
        out_specs=[out_spec(g) for g in groups],
        out_shape=[jax.ShapeDtypeStruct((batch, DSW_PATTERNS[g][1], seq // DSW_PATTERNS[g][1], 3 * d), bf16)
                   for g in groups],
        scratch_shapes=[pltpu.VMEM((tm, d), bf16), pltpu.VMEM((tm, tn), f32), pltpu.VMEM((tm, tn), f32),
                        pltpu.VMEM((DSW_HEADS, tm, dh), f32)],
        compiler_params=_params("arbitrary", "arbitrary"),
        name="dsw_in_proj",
    )(x, norm_g, mod_l, mod_l, w, gn, cos_t, sin_t)


def _dsw_attn_body(q_ref, kp_ref, kc_ref, vp_ref, vc_ref, o_ref, lse_ref, *, dh, span):
    blk = pl.program_id(2)
    p_len = DSW_BLOCK
    qi = lax.broadcasted_iota(jnp.int32, (p_len, 2 * p_len), 0)
    ki = lax.broadcasted_iota(jnp.int32, (p_len, 2 * p_len), 1)
    rel = qi + p_len - ki
    valid = (rel >= 0) & (rel <= span) & ((blk > 0) | (ki >= p_len))
    lane = lax.broadcasted_iota(jnp.int32, (p_len, LANES), 1)
    lse_tile = jnp.zeros((p_len, LANES), f32)
    hs = range(DSW_HEADS)
    scores = []
    for h in hs:
        sl = slice(h * dh, (h + 1) * dh)
        kcat = jnp.concatenate([kp_ref[:, sl], kc_ref[:, sl]], axis=0)
        scores.append(_dot_nt(q_ref[:, sl], kcat))
    for h in hs:
        sl = slice(h * dh, (h + 1) * dh)
        vcat = jnp.concatenate([vp_ref[:, sl], vc_ref[:, sl]], axis=0)
        s = jnp.where(valid, scores[h], -jnp.inf)
        mx = jnp.max(s, axis=-1, keepdims=True)
        p = jnp.exp(s - mx)
        l = jnp.sum(p, axis=-1, keepdims=True)
        o = _dot(p.astype(bf16), vcat) * (1.0 / l)
        o_ref[:, sl] = o.astype(bf16)
        lse_tile = jnp.where(lane == h, mx + jnp.log(l), lse_tile)
    lse_ref[...] = lse_tile


def _dsw_attention(qkv, window, dilation, *, d):
    batch, _, ts, _ = qkv.shape
    dh = d // DSW_HEADS
    nb = ts // DSW_BLOCK
    span = window // dilation

    def in_spec(which, prev):
        def imap(b, r, blk):
            return (b, r, jnp.maximum(blk - 1, 0) if prev else blk, which)
        return pl.BlockSpec((None, None, DSW_BLOCK, d), imap)

    body = functools.partial(_dsw_attn_body, dh=dh, span=span)
    return pl.pallas_call(
        body,
        grid=(batch, dilation, nb),
        in_specs=[in_spec(0, False), in_spec(1, True), in_spec(1, False),
                  in_spec(2, True), in_spec(2, False)],
        out_specs=[
            pl.BlockSpec((None, None, DSW_BLOCK, d), lambda b, r, blk: (b, r, blk, 0)),
            pl.BlockSpec((None, None, DSW_BLOCK, LANES), lambda b, r, blk: (b, r, blk, 0)),
        ],
        out_shape=[
            jax.ShapeDtypeStruct((batch, dilation, ts, d), bf16),
            jax.ShapeDtypeStruct((batch, dilation, ts, LANES), f32),
        ],
        compiler_params=_params("arbitrary", "arbitrary", "arbitrary"),
        name=f"dsw_attn_d{dilation}",
    )(qkv, qkv, qkv, qkv, qkv)


def _dsw_out_body(x_ref, o0_ref, o1_ref, o2_ref, l0_ref, l1_ref, l2_ref, w_ref, g_ref,
                  out_ref, lse_scr, nat_scr, om_scr, *, dh, tm):
    o_refs = (o0_ref, o1_ref, o2_ref)
    l_refs = (l0_ref, l1_ref, l2_ref)
    groups = range(len(DSW_PATTERNS))

    def token_rows(g, r):
        dil = DSW_PATTERNS[g][1]
        return pl.ds(r, tm // dil, stride=dil)

    for g in groups:
        for r in range(DSW_PATTERNS[g][1]):
            lse_scr[g, token_rows(g, r), :] = l_refs[g][r]
    lse = [lse_scr[g] for g in groups]
    mx = jnp.maximum(jnp.maximum(lse[0], lse[1]), lse[2])
    e = [jnp.exp(lse[g] - mx) for g in groups]
    inv = 1.0 / (e[0] + e[1] + e[2])
    wts = [e[g] * inv for g in groups]
    for h in range(DSW_HEADS):
        sl = slice(h * dh, (h + 1) * dh)
        om = None
        for g in groups:
            for r in range(DSW_PATTERNS[g][1]):
                nat_scr[g, token_rows(g, r), :] = o_refs[g][r, :, sl].astype(f32)
            term = wts[g][:, h:h + 1] * nat_scr[g]
            om = term if om is None else om + term
        om_scr[:, sl] = om.astype(bf16)
    out_ref[...] = x_ref[...] + g_ref[...] * _dot(om_scr[...], w_ref[...])


def _dsw_out_proj(x, outs, lses, w, mod_l, *, seq):
    m, d = x.shape
    tm = DSW_OUT_ROW_TILE
    tps = seq // tm
    dh = d // DSW_HEADS
    row = pl.BlockSpec((tm, d), lambda i: (i, 0))

    def stream_spec(g, width):
        dil = DSW_PATTERNS[g][1]
        return pl.BlockSpec((None, dil, tm // dil, width), lambda i: (i // tps, 0, i % tps, 0))

    groups = range(len(DSW_PATTERNS))
    body = functools.partial(_dsw_out_body, dh=dh, tm=tm)
    return pl.pallas_call(
        body,
        grid=(m // tm,),
        in_specs=[row] + [stream_spec(g, d) for g in groups] + [stream_spec(g, LANES) for g in groups]
                 + [pl.BlockSpec((d, d), lambda i: (0, 0)), _mod_spec(2, tps, d)],
        out_specs=row,
        out_shape=jax.ShapeDtypeStruct((m, d), f32),
        scratch_shapes=[pltpu.VMEM((len(DSW_PATTERNS), tm, LANES), f32),
                        pltpu.VMEM((len(DSW_PATTERNS), tm, dh), f32),
                        pltpu.VMEM((tm, d), bf16)],
        compiler_params=_params("arbitrary"),
        name="dsw_out_proj",
    )(x, *outs, *lses, w, mod_l)


def _ffn_body(x_ref, ng_ref, sh_ref, sc_ref, g_ref, wgu_ref, wd_ref, out_ref, h_scr, *, th):
    hidden = wd_ref.shape[0]
    h = _norm_modulate(x_ref[...], ng_ref[...], sh_ref[...], sc_ref[...])
    h_scr[...] = h.astype(bf16)
    hb = h_scr[...]
    acc = None
    act_prev = None
    for t in range(hidden // th + 1):
        if t * th < hidden:
            gate = _dot(hb, wgu_ref[:, t * th:(t + 1) * th])
            up = _dot(hb, wgu_ref[:, hidden + t * th:hidden + (t + 1) * th])
        if act_prev is not None:
            part = _dot(act_prev, wd_ref[(t - 1) * th:t * th, :])
            acc = part if acc is None else acc + part
        if t * th < hidden:
            act_prev = (_silu(gate) * up).astype(bf16)
    out_ref[...] = x_ref[...] + g_ref[...] * acc


def _ffn(x, mod_l, norm_g, w_gate_up, w_down, *, seq):
    m, d = x.shape
    tm = ROW_TILE
    tps = seq // tm
    hidden = w_down.shape[0]
    resident = pl.Buffered(1)
    body = functools.partial(_ffn_body, th=FFN_HIDDEN_TILE)
    return pl.pallas_call(
        body,
        grid=(m // tm,),
        in_specs=[
            pl.BlockSpec((tm, d), lambda i: (i, 0)),
            pl.BlockSpec((1, d), lambda i: (0, 0)),
            _mod_spec(3, tps, d),
            _mod_spec(4, tps, d),
            _mod_spec(5, tps, d),
            pl.BlockSpec((d, 2 * hidden), lambda i: (0, 0), pipeline_mode=resident),
            pl.BlockSpec((hidden, d), lambda i: (0, 0), pipeline_mode=resident),
        ],
        out_specs=pl.BlockSpec((tm, d), lambda i: (i, 0)),
        out_shape=jax.ShapeDtypeStruct((m, d), f32),
        scratch_shapes=[pltpu.VMEM((tm, d), bf16)],
        compiler_params=_params("arbitrary"),
        name="ffn",
    )(x, norm_g, mod_l, mod_l, mod_l, w_gate_up, w_down)


def _rope_tables(seq, dh):
    rope_dim = dh // 4
    half = rope_dim // 2
    inv = jnp.exp(-math.log(ROPE_THETA) * (2.0 * jnp.arange(half, dtype=f32) / rope_dim))
    ang = jnp.arange(seq, dtype=f32)[:, None] * inv[None, :]
    cos = jnp.cos(ang)
    sin = jnp.sin(ang)
    ones = jnp.ones((seq, dh // 2 - half), f32)
    zeros = jnp.zeros((seq, dh // 2 - half), f32)
    cos_t = jnp.concatenate([cos, ones, cos, ones], axis=-1)
    sin_t = jnp.concatenate([-sin, zeros, sin, zeros], axis=-1)
    return cos_t, sin_t


def _rope_lane_order(t, dh):
    half = (dh // 4) // 2
    return jnp.concatenate([t[..., :half], t[..., 2 * half:dh // 2 + half],
                            t[..., half:2 * half], t[..., dh // 2 + half:]], axis=-1)


def kernel(x, c, mod_w, mod_b, mix_norm_g, ffn_norm_g, gdn_w_in, gdn_conv_w, gdn_A_log,
           gdn_dt_bias, gdn_norm_g, gdn_w_out, dsw_w_in, dsw_q_norm_g, dsw_k_norm_g,
           dsw_w_out, ffn_w_gate_up, ffn_w_down):
    batch, seq, d = x.shape
    depth = mod_w.shape[0]
    m = batch * seq
    heads = GDN_HEADS
    dh = d // DSW_HEADS
    assert seq % ROW_TILE == 0 and seq % GDN_TIME_BLOCK == 0 and d == heads * LANES

    mod_rows = 16
    c_pad = jnp.zeros((mod_rows, d), f32).at[:batch].set(c)
    mod = _modulation(c_pad, mod_w, mod_b)[:, :batch]
    mod = mod.reshape(depth, batch, 6, 1, d)

    cos_t, sin_t = _rope_tables(seq, dh)
    xf = x.reshape(m, d)

    for layer in range(depth):
        mod_l = mod[layer]
        j = layer // N_MIXERS
        norm_g = mix_norm_g[layer].reshape(1, d)
        if layer % N_MIXERS == 0:
            n_conv = 3 * d
            w_in = gdn_w_in[j]
            w4 = w_in[:, :4 * d].astype(bf16)
            wba = jnp.zeros((d, LANES), f32).at[:, :2 * heads].set(w_in[:, 4 * d:]).astype(bf16)
            cw = jnp.zeros((GDN_CONV, 4 * d), f32).at[:, :n_conv].set(gdn_conv_w[j])
            gp = jnp.zeros((SUBLANES, LANES), f32)
            gp = gp.at[0, heads:2 * heads].set(gdn_A_log[j]).at[1, heads:2 * heads].set(gdn_dt_bias[j])
            qkvz, ba = _gdn_in_proj(xf, mod_l, norm_g, w4, wba, cw, gp, seq=seq)
            o = _gdn_core(qkvz, ba, gdn_norm_g[j].reshape(1, -1), batch=batch, seq=seq)
            xf = _out_proj(xf, o, gdn_w_out[j].astype(bf16), mod_l, 2, seq=seq)
        else:
            n_groups = len(DSW_PATTERNS)
            w5 = dsw_w_in[j].reshape(d, n_groups, 3, DSW_HEADS, dh)
            w5 = jnp.concatenate([_rope_lane_order(w5[:, :, :2], dh), w5[:, :, 2:]], axis=2)
            w_in = w5.reshape(d, -1).astype(bf16)
            gains = []
            for gi in range(n_groups):
                q_gain = _rope_lane_order(dsw_q_norm_g[j, gi], dh) * (dh ** -0.5)
                gains.append(jnp.tile(q_gain, DSW_HEADS))
                gains.append(jnp.tile(_rope_lane_order(dsw_k_norm_g[j, gi], dh), DSW_HEADS))
                gains.append(jnp.ones((d,), f32))
            gn = jnp.stack(gains).reshape(3 * n_groups, 1, d)
            qkvs = _dsw_in_proj(xf, mod_l, norm_g, w_in, gn, cos_t, sin_t, batch=batch, seq=seq)
            outs, lses = [], []
            for gi, (window, dilation) in enumerate(DSW_PATTERNS):
                o_g, lse_g = _dsw_attention(qkvs[gi], window, dilation, d=d)
                outs.append(o_g)
                lses.append(lse_g)
            xf = _dsw_out_proj(xf, outs, lses, dsw_w_out[j].astype(bf16), mod_l, seq=seq)
        xf = _ffn(xf, mod_l, ffn_norm_g[layer].reshape(1, d), ffn_w_gate_up[layer].astype(bf16),
                  ffn_w_down[layer].astype(bf16), seq=seq)
    return xf.reshape(batch, seq, d)
```

```python
import functools
import math

import jax
import jax.numpy as jnp
from jax import lax
from jax.experimental import pallas as pl
from jax.experimental.pallas import tpu as pltpu

EPS = 1e-6
N_MIXERS = 2
GDN_HEADS = 8
GDN_CONV = 4
GDN_CHUNK = 64
GDN_SOLVE_BLOCK = 16
GDN_SCAN_CHUNKS = 2
GDN_PREP_CHUNKS = 4
DSW_PATTERNS = ((128, 1), (512, 4), (2048, 16))
DSW_HEADS = 8
DSW_BLOCK = 128
ROPE_THETA = 500000.0

LANES = 128
SUBLANES = 8
VMEM_LIMIT_BYTES = 56 * 1024 * 1024

ROW_TILE = 1024
DSW_OUT_ROW_TILE = 512
GDN_TIME_BLOCK = 512
FFN_HIDDEN_TILE = 256

f32 = jnp.float32
bf16 = jnp.bfloat16


def _params(*sem):
    return pltpu.CompilerParams(dimension_semantics=sem, vmem_limit_bytes=VMEM_LIMIT_BYTES)


def _dot(a, b):
    return jnp.dot(a, b, preferred_element_type=f32)


def _dot_nt(a, b):
    return lax.dot_general(a, b, (((1,), (1,)), ((), ())), preferred_element_type=f32)


def _dot_tn(a, b):
    return lax.dot_general(a, b, (((0,), (0,)), ((), ())), preferred_element_type=f32)


def _dot_round(a, b):
    return _dot(a.astype(bf16), b.astype(bf16))


def _silu(x):
    return x * (0.5 + 0.5 * jnp.tanh(0.5 * x))


def _softplus(x):
    return jnp.maximum(x, 0.0) + jnp.log1p(jnp.exp(-jnp.abs(x)))


def _norm_modulate(x, gain, shift, scale):
    ms = jnp.mean(x * x, axis=-1, keepdims=True)
    y = x * lax.rsqrt(ms + EPS) * gain
    return y * (1.0 + scale) + shift


def _mod_body(c_ref, w_ref, b_ref, o_ref):
    c = c_ref[...]
    cond = _silu(c).astype(bf16)
    o_ref[...] = _dot(cond, w_ref[...].astype(bf16)) + b_ref[...]


def _modulation(c_pad, mod_w, mod_b):
    depth, d, n = mod_w.shape
    rows = c_pad.shape[0]
    tn = 1536
    return pl.pallas_call(
        _mod_body,
        grid=(depth, n // tn),
        in_specs=[
            pl.BlockSpec((rows, d), lambda l, j: (0, 0)),
            pl.BlockSpec((None, d, tn), lambda l, j: (l, 0, j)),
            pl.BlockSpec((None, 1, tn), lambda l, j: (l, 0, j)),
        ],
        out_specs=pl.BlockSpec((None, rows, tn), lambda l, j: (l, 0, j)),
        out_shape=jax.ShapeDtypeStruct((depth, rows, n), f32),
        compiler_params=_params("arbitrary", "arbitrary"),
        name="modulation",
    )(c_pad, mod_w, mod_b.reshape(depth, 1, n))


def _mod_spec(idx, tiles_per_seq, d):
    return pl.BlockSpec((None, None, 1, d), lambda i, *_: (i // tiles_per_seq, idx, 0, 0))


def _gdn_in_body(x_ref, ng_ref, sh_ref, sc_ref, w_ref, wba_ref, cw_ref, gp_ref,
                 out_ref, ba_ref, h_scr, ya_scr, yb_scr, tail_scr, *, tiles_per_seq, tm, dk, n_tiles):
    i = pl.program_id(0)
    j = pl.program_id(1)
    y_scr = (ya_scr, yb_scr)
    first = (i % tiles_per_seq) == 0
    n_conv_tiles = n_tiles - 1

    @pl.when((i == 0) & (j == 0))
    def _():
        tail_scr[...] = jnp.zeros(tail_scr.shape, f32)

    def epilogue(tile):
        y_ref = y_scr[tile % 2]
        if tile == n_conv_tiles:
            out_ref[...] = y_ref[SUBLANES:, :].astype(bf16)
            return
        y_ref[0:SUBLANES, :] = jnp.where(first, 0.0, tail_scr[tile])
        tail_scr[tile] = y_ref[tm:, :]
        post = dk ** -0.5 if tile == 0 else 1.0
        for h in range(GDN_HEADS):
            sl = slice(h * dk, (h + 1) * dk)
            acc = cw_ref[GDN_CONV - 1:GDN_CONV, sl] * y_ref[SUBLANES:, sl]
            for kk in range(GDN_CONV - 1):
                start = SUBLANES - (GDN_CONV - 1) + kk
                acc = acc + cw_ref[kk:kk + 1, sl] * y_ref[pl.ds(start, tm), sl]
            a = _silu(acc)
            if tile < 2:
                ss = jnp.sum(a * a, axis=-1, keepdims=True)
                a = a * (lax.rsqrt(ss + EPS) * post)
            out_ref[:, sl] = a.astype(bf16)

    for step in range(n_tiles + 1):
        @pl.when(j == step)
        def _(step=step):
            if step == 0:
                h = _norm_modulate(x_ref[...], ng_ref[...], sh_ref[...], sc_ref[...])
                hb = h.astype(bf16)
                h_scr[...] = hb
                ba = _dot(hb, wba_ref[...])
                lane = lax.broadcasted_iota(jnp.int32, ba.shape, 1)
                beta = jax.nn.sigmoid(ba)
                g = -jnp.exp(gp_ref[0:1, :]) * _softplus(ba + gp_ref[1:2, :])
                ba_ref[...] = jnp.where(lane < GDN_HEADS, beta, g)
            if step < n_tiles:
                y_scr[step % 2][SUBLANES:, :] = _dot(h_scr[...], w_ref[...])
            if step > 0:
                epilogue(step - 1)


def _gdn_in_proj(x, mod_l, norm_g, w4, wba, cw, gp, *, seq):
    m, d = x.shape
    tm = ROW_TILE
    tps = seq // tm
    n_tiles = 4
    tn = w4.shape[2]
    dk = tn // GDN_HEADS
    body = functools.partial(_gdn_in_body, tiles_per_seq=tps, tm=tm, dk=dk, n_tiles=n_tiles)

    def done(j):
        return jnp.clip(j - 1, 0, n_tiles - 1)

    return pl.pallas_call(
        body,
        grid=(m // tm, n_tiles + 1),
        in_specs=[
            pl.BlockSpec((tm, d), lambda i, j: (i, 0)),
            pl.BlockSpec((1, d), lambda i, j: (0, 0)),
            _mod_spec(0, tps, d),
            _mod_spec(1, tps, d),
            pl.BlockSpec((None, d, tn), lambda i, j: (jnp.minimum(j, n_tiles - 1), 0, 0)),
            pl.BlockSpec((d, LANES), lambda i, j: (0, 0)),
            pl.BlockSpec((None, GDN_CONV, tn), lambda i, j: (done(j), 0, 0)),
            pl.BlockSpec((SUBLANES, LANES), lambda i, j: (0, 0)),
        ],
        out_specs=[
            pl.BlockSpec((None, tm, tn), lambda i, j: (done(j), i, 0)),
            pl.BlockSpec((tm, LANES), lambda i, j: (i, 0)),
        ],
        out_shape=[
            jax.ShapeDtypeStruct((n_tiles, m, tn), bf16),
            jax.ShapeDtypeStruct((m, LANES), f32),
        ],
        scratch_shapes=[
            pltpu.VMEM((tm, d), bf16),
            pltpu.VMEM((tm + SUBLANES, tn), f32),
            pltpu.VMEM((tm + SUBLANES, tn), f32),
            pltpu.VMEM((n_tiles - 1, SUBLANES, tn), f32),
        ],
        compiler_params=_params("arbitrary", "arbitrary"),
        name="gdn_in_proj",
    )(x, norm_g, mod_l, mod_l, w4, wba, cw, gp)


def _gdn_core_body(q_ref, k_ref, v_ref, z_ref, ba_ref, ng_ref, o_ref,
                   s_scr, u_scr, w_scr, qd_scr, kt_scr, a_scr, gl_scr, *, tb, dk):
    c_len = GDN_CHUNK
    n_chunks = tb // c_len
    heads = GDN_HEADS
    hs = range(heads)

    @pl.when(pl.program_id(1) == 0)
    def _():
        s_scr[...] = jnp.zeros(s_scr.shape, f32)

    row = lax.broadcasted_iota(jnp.int32, (c_len, c_len), 0)
    col = lax.broadcasted_iota(jnp.int32, (c_len, c_len), 1)
    causal = row >= col
    strict = row > col
    tril_f = causal.astype(f32)
    eye_f = (row == col).astype(f32)
    diag_blk = (row // GDN_SOLVE_BLOCK) == (col // GDN_SOLVE_BLOCK)

    def prep(cp, carry):
        x, rhs, decay, where = [], [], [], []
        for ci in range(GDN_PREP_CHUNKS):
            c = cp * GDN_PREP_CHUNKS + ci
            rows = pl.ds(pl.multiple_of(c * c_len, c_len), c_len)
            g8 = pl.multiple_of(c * SUBLANES, SUBLANES)
            ba = ba_ref[rows, :]
            gc_all = jnp.dot(tril_f, ba, precision=lax.Precision.HIGHEST,
                             preferred_element_type=f32)
            gc_t = gc_all.T
            for h in hs:
                sl = slice(h * dk, (h + 1) * dk)
                q = q_ref[rows, sl]
                k = k_ref[rows, sl]
                beta = ba[:, h:h + 1]
                gc = gc_all[:, heads + h:heads + h + 1]
                gr = gc_t[heads + h:heads + h + 1, :]
                g_last = gc_all[c_len - 1:c_len, heads + h:heads + h + 1]
                dec = jnp.where(causal, jnp.exp(jnp.minimum(gc - gr, 0.0)), 0.0)
                eg = jnp.exp(gc)
                kf = k.astype(f32)
                kb = kf * beta
                kk = _dot_nt(kb.astype(bf16), k)
                x.append(jnp.where(strict, -(kk * dec), 0.0))
                rhs.append(jnp.concatenate([v_ref[rows, sl].astype(f32) * beta, kb * eg], axis=-1))
                decay.append(dec)
                where.append((h, rows, sl))
                qd_scr[h, rows, :] = (q.astype(f32) * eg).astype(bf16)
                kt_scr[h, rows, :] = (kf * jnp.exp(g_last - gc)).astype(bf16)
                gl_scr[h, pl.ds(g8, SUBLANES), :] = jnp.broadcast_to(jnp.exp(g_last), (SUBLANES, dk))
        items = range(len(x))
        for it in items:
            h, rows, sl = where[it]
            a_in = jnp.where(causal, _dot_nt(q_ref[rows, sl], k_ref[rows, sl]) * decay[it], 0.0)
            a_scr[h, rows, :] = a_in.astype(bf16)
        x_d = [jnp.where(diag_blk, x[it], 0.0) for it in items]
        d_inv = [eye_f + x_d[it] for it in items]
        xp = x_d
        for _ in range(int(math.log2(GDN_SOLVE_BLOCK)) - 1):
            xp = [_dot_round(xp[it], xp[it]) for it in items]
            d_inv = [d_inv[it] + _dot_round(d_inv[it], xp[it]) for it in items]
        mn = [_dot_round(d_inv[it], x[it] - x_d[it]) for it in items]
        o_inv = [eye_f + mn[it] for it in items]
        mp = mn
        for _ in range(int(math.log2(c_len // GDN_SOLVE_BLOCK)) - 1):
            mp = [_dot_round(mp[it], mp[it]) for it in items]
            o_inv = [o_inv[it] + _dot_round(o_inv[it], mp[it]) for it in items]
        t_inv = [_dot_round(o_inv[it], d_inv[it]) for it in items]
        sol = [_dot_round(t_inv[it], rhs[it]) for it in items]
        for it in items:
            h, rows, _ = where[it]
            u_scr[h, rows, :] = sol[it][:, :dk]
            w_scr[h, rows, :] = sol[it][:, dk:].astype(bf16)
        return carry

    lax.fori_loop(0, n_chunks // GDN_PREP_CHUNKS, prep, 0)

    ng = ng_ref[...]

    def scan_chunk(c, s):
        rows = pl.ds(pl.multiple_of(c * c_len, c_len), c_len)
        g8 = pl.multiple_of(c * SUBLANES, SUBLANES)
        sb = [s[h].astype(bf16) for h in hs]
        ws = [_dot(w_scr[h, rows, :], sb[h]) for h in hs]
        qs = [_dot(qd_scr[h, rows, :], sb[h]) for h in hs]
        v_new = [(u_scr[h, rows, :] - ws[h]).astype(bf16) for h in hs]
        s_new = [s[h] * gl_scr[h, pl.ds(g8, 1), :] + _dot_tn(kt_scr[h, rows, :], v_new[h]) for h in hs]
        o = [qs[h] + _dot(a_scr[h, rows, :], v_new[h]) for h in hs]
        for h in hs:
            sl = slice(h * dk, (h + 1) * dk)
            z = z_ref[rows, sl].astype(f32)
            ms = jnp.mean(o[h] * o[h], axis=-1, keepdims=True)
            o_ref[rows, sl] = (o[h] * lax.rsqrt(ms + EPS) * ng * _silu(z)).astype(bf16)
        return s_new

    def scan(cp, carry):
        s = [s_scr[h] for h in hs]
        for ci in range(GDN_SCAN_CHUNKS):
            s = scan_chunk(cp * GDN_SCAN_CHUNKS + ci, s)
        for h in hs:
            s_scr[h] = s[h]
        return carry

    lax.fori_loop(0, n_chunks // GDN_SCAN_CHUNKS, scan, 0)


def _gdn_core(qkvz, ba, norm_g, *, batch, seq):
    _, m, d = qkvz.shape
    dk = d // GDN_HEADS
    tb = GDN_TIME_BLOCK
    nt = seq // tb
    n_chunks = tb // GDN_CHUNK

    def col(jc):
        return pl.BlockSpec((None, tb, d), lambda b, t: (jc, b * nt + t, 0))

    body = functools.partial(_gdn_core_body, tb=tb, dk=dk)
    return pl.pallas_call(
        body,
        grid=(batch, nt),
        in_specs=[col(0), col(1), col(2), col(3),
                  pl.BlockSpec((tb, LANES), lambda b, t: (b * nt + t, 0)),
                  pl.BlockSpec((1, dk), lambda b, t: (0, 0))],
        out_specs=pl.BlockSpec((tb, d), lambda b, t: (b * nt + t, 0)),
        out_shape=jax.ShapeDtypeStruct((m, d), bf16),
        scratch_shapes=[
            pltpu.VMEM((GDN_HEADS, dk, dk), f32),
            pltpu.VMEM((GDN_HEADS, tb, dk), f32),
            pltpu.VMEM((GDN_HEADS, tb, dk), bf16),
            pltpu.VMEM((GDN_HEADS, tb, dk), bf16),
            pltpu.VMEM((GDN_HEADS, tb, dk), bf16),
            pltpu.VMEM((GDN_HEADS, tb, GDN_CHUNK), bf16),
            pltpu.VMEM((GDN_HEADS, n_chunks * SUBLANES, dk), f32),
        ],
        compiler_params=_params("arbitrary", "arbitrary"),
        name="gdn_core",
    )(qkvz, qkvz, qkvz, qkvz, ba, norm_g)


def _out_proj_body(x_ref, o_ref, w_ref, g_ref, out_ref):
    out_ref[...] = x_ref[...] + g_ref[...] * _dot(o_ref[...], w_ref[...])


def _out_proj(x, o, w, mod_l, gate_idx, *, seq):
    m, d = x.shape
    tm = ROW_TILE
    tps = seq // tm
    return pl.pallas_call(
        _out_proj_body,
        grid=(m // tm,),
        in_specs=[
            pl.BlockSpec((tm, d), lambda i: (i, 0)),
            pl.BlockSpec((tm, d), lambda i: (i, 0)),
            pl.BlockSpec((d, d), lambda i: (0, 0)),
            _mod_spec(gate_idx, tps, d),
        ],
        out_specs=pl.BlockSpec((tm, d), lambda i: (i, 0)),
        out_shape=jax.ShapeDtypeStruct((m, d), f32),
        compiler_params=_params("arbitrary"),
        name="out_proj",
    )(x, o, w, mod_l)


def _dsw_in_body(x_ref, ng_ref, sh_ref, sc_ref, w_ref, gn_ref, cos_ref, sin_ref,
                 o0_ref, o1_ref, o2_ref, h_scr, ya_scr, yb_scr, slab_scr, *, dh, tm, n_tiles):
    j = pl.program_id(1)
    y_scr = (ya_scr, yb_scr)
    out_refs = (o0_ref, o1_ref, o2_ref)

    def write_head(g, h, val):
        dil = DSW_PATTERNS[g][1]
        n = tm // dil
        sl = slice(h * dh, (h + 1) * dh)
        if dil == 1:
            out_refs[g][0, :, sl] = val.astype(bf16)
        else:
            slab_scr[h] = val
            for r in range(dil):
                out_refs[g][r, :, sl] = slab_scr[h, pl.ds(r, n, stride=dil), :].astype(bf16)

    def epilogue(tile):
        g, which = divmod(tile, 3)
        y_ref = y_scr[tile % 2]
        if which == 2:
            for h in range(DSW_HEADS):
                write_head(g, h, y_ref[:, h * dh:(h + 1) * dh])
            return
        cos_t = cos_ref[...]
        sin_t = sin_ref[...]
        for h in range(DSW_HEADS):
            sl = slice(h * dh, (h + 1) * dh)
            yh = y_ref[:, sl]
            ms = jnp.mean(yh * yh, axis=-1, keepdims=True)
            yn = yh * lax.rsqrt(ms + EPS) * gn_ref[:, sl]
            write_head(g, h, yn * cos_t + pltpu.roll(yn, dh // 2, 1) * sin_t)

    for step in range(n_tiles + 1):
        @pl.when(j == step)
        def _(step=step):
            if step == 0:
                h = _norm_modulate(x_ref[...], ng_ref[...], sh_ref[...], sc_ref[...])
                h_scr[...] = h.astype(bf16)
            if step < n_tiles:
                y_scr[step % 2][...] = _dot(h_scr[...], w_ref[...])
            if step > 0:
                epilogue(step - 1)


def _dsw_in_proj(x, mod_l, norm_g, w, gn, cos_t, sin_t, *, batch, seq):
    m, d = x.shape
    tm = ROW_TILE
    tps = seq // tm
    n_tiles, _, tn = w.shape
    dh = d // DSW_HEADS
    body = functools.partial(_dsw_in_body, dh=dh, tm=tm, n_tiles=n_tiles)
    rope_spec = pl.BlockSpec((tm, dh), lambda i, j: (i % tps, 0))

    def out_spec(g):
        dil = DSW_PATTERNS[g][1]
        return pl.BlockSpec((None, dil, None, tm // dil, tn),
                            lambda i, j: (i // tps, 0, jnp.clip(j - 1 - 3 * g, 0, 2), i % tps, 0))

    groups = range(len(DSW_PATTERNS))
    return pl.pallas_call(
        body,
        grid=(m // tm, n_tiles + 1),
        in_specs=[
            pl.BlockSpec((tm, d), lambda i, j: (i, 0)),
            pl.BlockSpec((1, d), lambda i, j: (0, 0)),
            _mod_spec(0, tps, d),
            _mod_spec(1, tps, d),
            pl.BlockSpec((None, d, tn), lambda i, j: (jnp.minimum(j, n_tiles - 1), 0, 0)),
            pl.BlockSpec((None, 1, tn), lambda i, j: (jnp.maximum(j - 1, 0), 0, 0)),
            rope_spec, rope_spec,
        ],
        out_specs=[out_spec(g) for g in groups],
        out_shape=[jax.ShapeDtypeStruct((batch, DSW_PATTERNS[g][1], 3, seq // DSW_PATTERNS[g][1], d), bf16)
                   for g in groups],
        scratch_shapes=[pltpu.VMEM((tm, d), bf16), pltpu.VMEM((tm, tn), f32), pltpu.VMEM((tm, tn), f32),
                        pltpu.VMEM((DSW_HEADS, tm, dh), f32)],
        compiler_params=_params("arbitrary", "arbitrary"),
        name="dsw_in_proj",
    )(x, norm_g, mod_l, mod_l, w, gn, cos_t, sin_t)


def _dsw_attn_body(q_ref, kp_ref, kc_ref, vp_ref, vc_ref, o_ref, lse_ref, *, dh, span):
    blk = pl.program_id(2)
    p_len = DSW_BLOCK
    qi = lax.broadcasted_iota(jnp.int32, (p_len, 2 * p_len), 0)
    ki = lax.broadcasted_iota(jnp.int32, (p_len, 2 * p_len), 1)
    rel = qi + p_len - ki
    valid = (rel >= 0) & (rel <= span) & ((blk > 0) | (ki >= p_len))
    lane = lax.broadcasted_iota(jnp.int32, (p_len, LANES), 1)
    lse_tile = jnp.zeros((p_len, LANES), f32)
    hs = range(DSW_HEADS)
    scores = []
    for h in hs:
        sl = slice(h * dh, (h + 1) * dh)
        kcat = jnp.concatenate([kp_ref[:, sl], kc_ref[:, sl]], axis=0)
        scores.append(_dot_nt(q_ref[:, sl], kcat))
    for h in hs:
        sl = slice(h * dh, (h + 1) * dh)
        vcat = jnp.concatenate([vp_ref[:, sl], vc_ref[:, sl]], axis=0)
        s = jnp.where(valid, scores[h], -jnp.inf)
        mx = jnp.max(s, axis=-1, keepdims=True)
        p = jnp.exp(s - mx)
        l = jnp.sum(p, axis=-1, keepdims=True)
        o = _dot(p.astype(bf16), vcat) * (1.0 / l)
        o_ref[:, sl] = o.astype(bf16)
        lse_tile = jnp.where(lane == h, mx + jnp.log(l), lse_tile)
    lse_ref[...] = lse_tile


def _dsw_attention(qkv, window, dilation, *, d):
    batch, _, _, ts, _ = qkv.shape
    dh = d // DSW_HEADS
    nb = ts // DSW_BLOCK
    span = window // dilation

    def in_spec(which, prev):
        def imap(b, r, blk):
            return (b, r, which, jnp.maximum(blk - 1, 0) if prev else blk, 0)
        return pl.BlockSpec((None, None, None, DSW_BLOCK, d), imap)

    body = functools.partial(_dsw_attn_body, dh=dh, span=span)
    return pl.pallas_call(
        body,
        grid=(batch, dilation, nb),
        in_specs=[in_spec(0, False), in_spec(1, True), in_spec(1, False),
                  in_spec(2, True), in_spec(2, False)],
        out_specs=[
            pl.BlockSpec((None, None, DSW_BLOCK, d), lambda b, r, blk: (b, r, blk, 0)),
            pl.BlockSpec((None, None, DSW_BLOCK, LANES), lambda b, r, blk: (b, r, blk, 0)),
        ],
        out_shape=[
            jax.ShapeDtypeStruct((batch, dilation, ts, d), bf16),
            jax.ShapeDtypeStruct((batch, dilation, ts, LANES), f32),
        ],
        compiler_params=_params("arbitrary", "arbitrary", "arbitrary"),
        name=f"dsw_attn_d{dilation}",
    )(qkv, qkv, qkv, qkv, qkv)


def _dsw_out_body(x_ref, o0_ref, o1_ref, o2_ref, l0_ref, l1_ref, l2_ref, w_ref, g_ref,
                  out_ref, lse_scr, nat_scr, om_scr, *, dh, tm):
    o_refs = (o0_ref, o1_ref, o2_ref)
    l_refs = (l0_ref, l1_ref, l2_ref)
    groups = range(len(DSW_PATTERNS))

    def token_rows(g, r):
        dil = DSW_PATTERNS[g][1]
        return pl.ds(r, tm // dil, stride=dil)

    for g in groups:
        for r in range(DSW_PATTERNS[g][1]):
            lse_scr[g, token_rows(g, r), :] = l_refs[g][r]
    lse = [lse_scr[g] for g in groups]
    mx = jnp.maximum(jnp.maximum(lse[0], lse[1]), lse[2])
    e = [jnp.exp(lse[g] - mx) for g in groups]
    inv = 1.0 / (e[0] + e[1] + e[2])
    wts = [e[g] * inv for g in groups]
    for h in range(DSW_HEADS):
        sl = slice(h * dh, (h + 1) * dh)
        om = None
        for g in groups:
            for r in range(DSW_PATTERNS[g][1]):
                nat_scr[g, token_rows(g, r), :] = o_refs[g][r, :, sl].astype(f32)
            term = wts[g][:, h:h + 1] * nat_scr[g]
            om = term if om is None else om + term
        om_scr[:, sl] = om.astype(bf16)
    out_ref[...] = x_ref[...] + g_ref[...] * _dot(om_scr[...], w_ref[...])


def _dsw_out_proj(x, outs, lses, w, mod_l, *, seq):
    m, d = x.shape
    tm = DSW_OUT_ROW_TILE
    tps = seq // tm
    dh = d // DSW_HEADS
    row = pl.BlockSpec((tm, d), lambda i: (i, 0))

    def stream_spec(g, width):
        dil = DSW_PATTERNS[g][1]
        return pl.BlockSpec((None, dil, tm // dil, width), lambda i: (i // tps, 0, i % tps, 0))

    groups = range(len(DSW_PATTERNS))
    body = functools.partial(_dsw_out_body, dh=dh, tm=tm)
    return pl.pallas_call(
        body,
        grid=(m // tm,),
        in_specs=[row] + [stream_spec(g, d) for g in groups] + [stream_spec(g, LANES) for g in groups]
                 + [pl.BlockSpec((d, d), lambda i: (0, 0)), _mod_spec(2, tps, d)],
        out_specs=row,
        out_shape=jax.ShapeDtypeStruct((m, d), f32),
        scratch_shapes=[pltpu.VMEM((len(DSW_PATTERNS), tm, LANES), f32),
                        pltpu.VMEM((len(DSW_PATTERNS), tm, dh), f32),
                        pltpu.VMEM((tm, d), bf16)],
        compiler_params=_params("arbitrary"),
        name="dsw_out_proj",
    )(x, *outs, *lses, w, mod_l)


def _ffn_body(x_ref, ng_ref, sh_ref, sc_ref, g_ref, wgu_ref, wd_ref, out_ref, h_scr, *, th):
    hidden = wd_ref.shape[0]
    h = _norm_modulate(x_ref[...], ng_ref[...], sh_ref[...], sc_ref[...])
    h_scr[...] = h.astype(bf16)
    hb = h_scr[...]
    acc = None
    act_prev = None
    for t in range(hidden // th + 1):
        if t * th < hidden:
            gate = _dot(hb, wgu_ref[:, t * th:(t + 1) * th])
            up = _dot(hb, wgu_ref[:, hidden + t * th:hidden + (t + 1) * th])
        if act_prev is not None:
            part = _dot(act_prev, wd_ref[(t - 1) * th:t * th, :])
            acc = part if acc is None else acc + part
        if t * th < hidden:
            act_prev = (_silu(gate) * up).astype(bf16)
    out_ref[...] = x_ref[...] + g_ref[...] * acc


def _ffn(x, mod_l, norm_g, w_gate_up, w_down, *, seq):
    m, d = x.shape
    tm = ROW_TILE
    tps = seq // tm
    hidden = w_down.shape[0]
    resident = pl.Buffered(1)
    body = functools.partial(_ffn_body, th=FFN_HIDDEN_TILE)
    return pl.pallas_call(
        body,
        grid=(m // tm,),
        in_specs=[
            pl.BlockSpec((tm, d), lambda i: (i, 0)),
            pl.BlockSpec((1, d), lambda i: (0, 0)),
            _mod_spec(3, tps, d),
            _mod_spec(4, tps, d),
            _mod_spec(5, tps, d),
            pl.BlockSpec((d, 2 * hidden), lambda i: (0, 0), pipeline_mode=resident),
            pl.BlockSpec((hidden, d), lambda i: (0, 0), pipeline_mode=resident),
        ],
        out_specs=pl.BlockSpec((tm, d), lambda i: (i, 0)),
        out_shape=jax.ShapeDtypeStruct((m, d), f32),
        scratch_shapes=[pltpu.VMEM((tm, d), bf16)],
        compiler_params=_params("arbitrary"),
        name="ffn",
    )(x, norm_g, mod_l, mod_l, mod_l, w_gate_up, w_down)


def _rope_tables(seq, dh):
    rope_dim = dh // 4
    half = rope_dim // 2
    inv = jnp.exp(-math.log(ROPE_THETA) * (2.0 * jnp.arange(half, dtype=f32) / rope_dim))
    ang = jnp.arange(seq, dtype=f32)[:, None] * inv[None, :]
    cos = jnp.cos(ang)
    sin = jnp.sin(ang)
    ones = jnp.ones((seq, dh // 2 - half), f32)
    zeros = jnp.zeros((seq, dh // 2 - half), f32)
    cos_t = jnp.concatenate([cos, ones, cos, ones], axis=-1)
    sin_t = jnp.concatenate([-sin, zeros, sin, zeros], axis=-1)
    return cos_t, sin_t


def _rope_lane_order(t, dh):
    half = (dh // 4) // 2
    return jnp.concatenate([t[..., :half], t[..., 2 * half:dh // 2 + half],
                            t[..., half:2 * half], t[..., dh // 2 + half:]], axis=-1)


def kernel(x, c, mod_w, mod_b, mix_norm_g, ffn_norm_g, gdn_w_in, gdn_conv_w, gdn_A_log,
           gdn_dt_bias, gdn_norm_g, gdn_w_out, dsw_w_in, dsw_q_norm_g, dsw_k_norm_g,
           dsw_w_out, ffn_w_gate_up, ffn_w_down):
    batch, seq, d = x.shape
    depth = mod_w.shape[0]
    m = batch * seq
    heads = GDN_HEADS
    dh = d // DSW_HEADS
    assert seq % ROW_TILE == 0 and seq % GDN_TIME_BLOCK == 0 and d == heads * LANES

    mod_rows = 16
    c_pad = jnp.zeros((mod_rows, d), f32).at[:batch].set(c)
    mod = _modulation(c_pad, mod_w, mod_b)[:, :batch]
    mod = mod.reshape(depth, batch, 6, 1, d)

    cos_t, sin_t = _rope_tables(seq, dh)
    xf = x.reshape(m, d)

    for layer in range(depth):
        mod_l = mod[layer]
        j = layer // N_MIXERS
        norm_g = mix_norm_g[layer].reshape(1, d)
        if layer % N_MIXERS == 0:
            n_conv = 3 * d
            w_in = gdn_w_in[j]
            w4 = w_in[:, :4 * d].reshape(d, 4, d).transpose(1, 0, 2).astype(bf16)
            wba = jnp.zeros((d, LANES), f32).at[:, :2 * heads].set(w_in[:, 4 * d:]).astype(bf16)
            cw = jnp.zeros((GDN_CONV, 4 * d), f32).at[:, :n_conv].set(gdn_conv_w[j])
            cw = cw.reshape(GDN_CONV, 4, d).transpose(1, 0, 2)
            gp = jnp.zeros((SUBLANES, LANES), f32)
            gp = gp.at[0, heads:2 * heads].set(gdn_A_log[j]).at[1, heads:2 * heads].set(gdn_dt_bias[j])
            qkvz, ba = _gdn_in_proj(xf, mod_l, norm_g, w4, wba, cw, gp, seq=seq)
            o = _gdn_core(qkvz, ba, gdn_norm_g[j].reshape(1, -1), batch=batch, seq=seq)
            xf = _out_proj(xf, o, gdn_w_out[j].astype(bf16), mod_l, 2, seq=seq)
        else:
            n_groups = len(DSW_PATTERNS)
            w5 = dsw_w_in[j].reshape(d, n_groups, 3, DSW_HEADS, dh)
            w5 = jnp.concatenate([_rope_lane_order(w5[:, :, :2], dh), w5[:, :, 2:]], axis=2)
            w_in = w5.reshape(d, 3 * n_groups, d).transpose(1, 0, 2).astype(bf16)
            gains = []
            for gi in range(n_groups):
                q_gain = _rope_lane_order(dsw_q_norm_g[j, gi], dh) * (dh ** -0.5)
                gains.append(jnp.tile(q_gain, DSW_HEADS))
                gains.append(jnp.tile(_rope_lane_order(dsw_k_norm_g[j, gi], dh), DSW_HEADS))
                gains.append(jnp.ones((d,), f32))
            gn = jnp.stack(gains).reshape(3 * n_groups, 1, d)
            qkvs = _dsw_in_proj(xf, mod_l, norm_g, w_in, gn, cos_t, sin_t, batch=batch, seq=seq)
            outs, lses = [], []
            for gi, (window, dilation) in enumerate(DSW_PATTERNS):
                o_g, lse_g = _dsw_attention(qkvs[gi], window, dilation, d=d)
                outs.append(o_g)
                lses.append(lse_g)
            xf = _dsw_out_proj(xf, outs, lses, dsw_w_out[j].astype(bf16), mod_l, seq=seq)
        xf = _ffn(xf, mod_l, ffn_norm_g[layer].reshape(1, d), ffn_w_gate_up[layer].astype(bf16),
                  ffn_w_down[layer].astype(bf16), seq=seq)
    return xf.reshape(batch, seq, d)
```

```python
import functools
import math

import jax
import jax.numpy as jnp
from jax import lax
from jax.experimental import pallas as pl
from jax.experimental.pallas import tpu as pltpu

EPS = 1e-6
N_MIXERS = 2
GDN_HEADS = 8
GDN_CONV = 4
GDN_CHUNK = 64
GDN_SOLVE_BLOCK = 16
GDN_SCAN_CHUNKS = 2
GDN_PREP_CHUNKS = 4
DSW_PATTERNS = ((128, 1), (512, 4), (2048, 16))
DSW_HEADS = 8
DSW_BLOCK = 128
ROPE_THETA = 500000.0

LANES = 128
SUBLANES = 8
VMEM_LIMIT_BYTES = 56 * 1024 * 1024

ROW_TILE = 1024
DSW_OUT_ROW_TILE = 512
GDN_TIME_BLOCK = 512
FFN_HIDDEN_TILE = 256

f32 = jnp.float32
bf16 = jnp.bfloat16


def _params(*sem):
    return pltpu.CompilerParams(dimension_semantics=sem, vmem_limit_bytes=VMEM_LIMIT_BYTES)


def _dot(a, b):
    return jnp.dot(a, b, preferred_element_type=f32)


def _dot_nt(a, b):
    return lax.dot_general(a, b, (((1,), (1,)), ((), ())), preferred_element_type=f32)


def _dot_tn(a, b):
    return lax.dot_general(a, b, (((0,), (0,)), ((), ())), preferred_element_type=f32)


def _dot_round(a, b):
    return _dot(a.astype(bf16), b.astype(bf16))


def _silu(x):
    return x * (0.5 + 0.5 * jnp.tanh(0.5 * x))


def _softplus(x):
    return jnp.maximum(x, 0.0) + jnp.log1p(jnp.exp(-jnp.abs(x)))


def _norm_modulate(x, gain, shift, scale):
    ms = jnp.mean(x * x, axis=-1, keepdims=True)
    y = x * lax.rsqrt(ms + EPS) * gain
    return y * (1.0 + scale) + shift


def _mod_body(c_ref, w_ref, b_ref, o_ref):
    c = c_ref[...]
    cond = _silu(c).astype(bf16)
    o_ref[...] = _dot(cond, w_ref[...].astype(bf16)) + b_ref[...]


def _modulation(c_pad, mod_w, mod_b):
    depth, d, n = mod_w.shape
    rows = c_pad.shape[0]
    tn = 1536
    return pl.pallas_call(
        _mod_body,
        grid=(depth, n // tn),
        in_specs=[
            pl.BlockSpec((rows, d), lambda l, j: (0, 0)),
            pl.BlockSpec((None, d, tn), lambda l, j: (l, 0, j)),
            pl.BlockSpec((None, 1, tn), lambda l, j: (l, 0, j)),
        ],
        out_specs=pl.BlockSpec((None, rows, tn), lambda l, j: (l, 0, j)),
        out_shape=jax.ShapeDtypeStruct((depth, rows, n), f32),
        compiler_params=_params("arbitrary", "arbitrary"),
        name="modulation",
    )(c_pad, mod_w, mod_b.reshape(depth, 1, n))


def _mod_spec(idx, tiles_per_seq, d):
    return pl.BlockSpec((None, None, 1, d), lambda i, *_: (i // tiles_per_seq, idx, 0, 0))


def _gdn_in_body(x_ref, ng_ref, sh_ref, sc_ref, w_ref, wba_ref, cw_ref, gp_ref,
                 out_ref, ba_ref, h_scr, ya_scr, yb_scr, tail_scr, *, tiles_per_seq, tm, dk):
    i = pl.program_id(0)
    n_tiles = w_ref.shape[0]
    y_scr = (ya_scr, yb_scr)
    first = (i % tiles_per_seq) == 0
    n_conv_tiles = n_tiles - 1

    @pl.when(i == 0)
    def _():
        tail_scr[...] = jnp.zeros(tail_scr.shape, f32)

    def epilogue(tile):
        y_ref = y_scr[tile % 2]
        if tile == n_conv_tiles:
            out_ref[tile] = y_ref[SUBLANES:, :].astype(bf16)
            return
        y_ref[0:SUBLANES, :] = jnp.where(first, 0.0, tail_scr[tile])
        tail_scr[tile] = y_ref[tm:, :]
        post = dk ** -0.5 if tile == 0 else 1.0
        for h in range(GDN_HEADS):
            sl = slice(h * dk, (h + 1) * dk)
            acc = cw_ref[tile, GDN_CONV - 1:GDN_CONV, sl] * y_ref[SUBLANES:, sl]
            for kk in range(GDN_CONV - 1):
                start = SUBLANES - (GDN_CONV - 1) + kk
                acc = acc + cw_ref[tile, kk:kk + 1, sl] * y_ref[pl.ds(start, tm), sl]
            a = _silu(acc)
            if tile < 2:
                ss = jnp.sum(a * a, axis=-1, keepdims=True)
                a = a * (lax.rsqrt(ss + EPS) * post)
            out_ref[tile, :, sl] = a.astype(bf16)

    def matmul(tile):
        y_scr[tile % 2][SUBLANES:, :] = _dot(h_scr[...], w_ref[tile])

    h = _norm_modulate(x_ref[...], ng_ref[...], sh_ref[...], sc_ref[...])
    hb = h.astype(bf16)
    h_scr[...] = hb
    matmul(0)
    ba = _dot(hb, wba_ref[...])
    lane = lax.broadcasted_iota(jnp.int32, ba.shape, 1)
    beta = jax.nn.sigmoid(ba)
    g = -jnp.exp(gp_ref[0:1, :]) * _softplus(ba + gp_ref[1:2, :])
    ba_ref[...] = jnp.where(lane < GDN_HEADS, beta, g)
    for tile in range(n_tiles):
        if tile + 1 < n_tiles:
            matmul(tile + 1)
        epilogue(tile)


def _gdn_in_proj(x, mod_l, norm_g, w4, wba, cw, gp, *, seq):
    m, d = x.shape
    tm = ROW_TILE
    tps = seq // tm
    n_tiles, _, tn = w4.shape
    dk = tn // GDN_HEADS
    body = functools.partial(_gdn_in_body, tiles_per_seq=tps, tm=tm, dk=dk)
    resident = pl.Buffered(1)
    return pl.pallas_call(
        body,
        grid=(m // tm,),
        in_specs=[
            pl.BlockSpec((tm, d), lambda i: (i, 0)),
            pl.BlockSpec((1, d), lambda i: (0, 0)),
            _mod_spec(0, tps, d),
            _mod_spec(1, tps, d),
            pl.BlockSpec((n_tiles, d, tn), lambda i: (0, 0, 0), pipeline_mode=resident),
            pl.BlockSpec((d, LANES), lambda i: (0, 0)),
            pl.BlockSpec((n_tiles, GDN_CONV, tn), lambda i: (0, 0, 0)),
            pl.BlockSpec((SUBLANES, LANES), lambda i: (0, 0)),
        ],
        out_specs=[
            pl.BlockSpec((n_tiles, tm, tn), lambda i: (0, i, 0)),
            pl.BlockSpec((tm, LANES), lambda i: (i, 0)),
        ],
        out_shape=[
            jax.ShapeDtypeStruct((n_tiles, m, tn), bf16),
            jax.ShapeDtypeStruct((m, LANES), f32),
        ],
        scratch_shapes=[
            pltpu.VMEM((tm, d), bf16),
            pltpu.VMEM((tm + SUBLANES, tn), f32),
            pltpu.VMEM((tm + SUBLANES, tn), f32),
            pltpu.VMEM((n_tiles - 1, SUBLANES, tn), f32),
        ],
        compiler_params=_params("arbitrary"),
        name="gdn_in_proj",
    )(x, norm_g, mod_l, mod_l, w4, wba, cw, gp)


def _gdn_core_body(q_ref, k_ref, v_ref, z_ref, ba_ref, ng_ref, o_ref,
                   s_scr, u_scr, w_scr, qd_scr, kt_scr, a_scr, gl_scr, *, tb, dk):
    c_len = GDN_CHUNK
    n_chunks = tb // c_len
    heads = GDN_HEADS
    hs = range(heads)

    @pl.when(pl.program_id(1) == 0)
    def _():
        s_scr[...] = jnp.zeros(s_scr.shape, f32)

    row = lax.broadcasted_iota(jnp.int32, (c_len, c_len), 0)
    col = lax.broadcasted_iota(jnp.int32, (c_len, c_len), 1)
    causal = row >= col
    strict = row > col
    tril_f = causal.astype(f32)
    eye_f = (row == col).astype(f32)
    diag_blk = (row // GDN_SOLVE_BLOCK) == (col // GDN_SOLVE_BLOCK)

    def prep(cp, carry):
        x, rhs, decay, where = [], [], [], []
        for ci in range(GDN_PREP_CHUNKS):
            c = cp * GDN_PREP_CHUNKS + ci
            rows = pl.ds(pl.multiple_of(c * c_len, c_len), c_len)
            g8 = pl.multiple_of(c * SUBLANES, SUBLANES)
            ba = ba_ref[rows, :]
            gc_all = jnp.dot(tril_f, ba, precision=lax.Precision.HIGHEST,
                             preferred_element_type=f32)
            gc_t = gc_all.T
            for h in hs:
                sl = slice(h * dk, (h + 1) * dk)
                q = q_ref[rows, sl]
                k = k_ref[rows, sl]
                beta = ba[:, h:h + 1]
                gc = gc_all[:, heads + h:heads + h + 1]
                gr = gc_t[heads + h:heads + h + 1, :]
                g_last = gc_all[c_len - 1:c_len, heads + h:heads + h + 1]
                dec = jnp.where(causal, jnp.exp(jnp.minimum(gc - gr, 0.0)), 0.0)
                eg = jnp.exp(gc)
                kf = k.astype(f32)
                kb = kf * beta
                kk = _dot_nt(kb.astype(bf16), k)
                x.append(jnp.where(strict, -(kk * dec), 0.0))
                rhs.append(jnp.concatenate([v_ref[rows, sl].astype(f32) * beta, kb * eg], axis=-1))
                decay.append(dec)
                where.append((h, rows, sl))
                qd_scr[h, rows, :] = (q.astype(f32) * eg).astype(bf16)
                kt_scr[h, rows, :] = (kf * jnp.exp(g_last - gc)).astype(bf16)
                gl_scr[h, pl.ds(g8, SUBLANES), :] = jnp.broadcast_to(jnp.exp(g_last), (SUBLANES, dk))
        items = range(len(x))
        for it in items:
            h, rows, sl = where[it]
            a_in = jnp.where(causal, _dot_nt(q_ref[rows, sl], k_ref[rows, sl]) * decay[it], 0.0)
            a_scr[h, rows, :] = a_in.astype(bf16)
        x_d = [jnp.where(diag_blk, x[it], 0.0) for it in items]
        d_inv = [eye_f + x_d[it] for it in items]
        xp = x_d
        for _ in range(int(math.log2(GDN_SOLVE_BLOCK)) - 1):
            xp = [_dot_round(xp[it], xp[it]) for it in items]
            d_inv = [d_inv[it] + _dot_round(d_inv[it], xp[it]) for it in items]
        mn = [_dot_round(d_inv[it], x[it] - x_d[it]) for it in items]
        o_inv = [eye_f + mn[it] for it in items]
        mp = mn
        for _ in range(int(math.log2(c_len // GDN_SOLVE_BLOCK)) - 1):
            mp = [_dot_round(mp[it], mp[it]) for it in items]
            o_inv = [o_inv[it] + _dot_round(o_inv[it], mp[it]) for it in items]
        t_inv = [_dot_round(o_inv[it], d_inv[it]) for it in items]
        sol = [_dot_round(t_inv[it], rhs[it]) for it in items]
        for it in items:
            h, rows, _ = where[it]
            u_scr[h, rows, :] = sol[it][:, :dk]
            w_scr[h, rows, :] = sol[it][:, dk:].astype(bf16)
        return carry

    lax.fori_loop(0, n_chunks // GDN_PREP_CHUNKS, prep, 0)

    ng = ng_ref[...]

    def scan_chunk(c, s):
        rows = pl.ds(pl.multiple_of(c * c_len, c_len), c_len)
        g8 = pl.multiple_of(c * SUBLANES, SUBLANES)
        sb = [s[h].astype(bf16) for h in hs]
        ws = [_dot(w_scr[h, rows, :], sb[h]) for h in hs]
        qs = [_dot(qd_scr[h, rows, :], sb[h]) for h in hs]
        v_new = [(u_scr[h, rows, :] - ws[h]).astype(bf16) for h in hs]
        s_new = [s[h] * gl_scr[h, pl.ds(g8, 1), :] + _dot_tn(kt_scr[h, rows, :], v_new[h]) for h in hs]
        o = [qs[h] + _dot(a_scr[h, rows, :], v_new[h]) for h in hs]
        for h in hs:
            sl = slice(h * dk, (h + 1) * dk)
            z = z_ref[rows, sl].astype(f32)
            ms = jnp.mean(o[h] * o[h], axis=-1, keepdims=True)
            o_ref[rows, sl] = (o[h] * lax.rsqrt(ms + EPS) * ng * _silu(z)).astype(bf16)
        return s_new

    def scan(cp, carry):
        s = [s_scr[h] for h in hs]
        for ci in range(GDN_SCAN_CHUNKS):
            s = scan_chunk(cp * GDN_SCAN_CHUNKS + ci, s)
        for h in hs:
            s_scr[h] = s[h]
        return carry

    lax.fori_loop(0, n_chunks // GDN_SCAN_CHUNKS, scan, 0)


def _gdn_core(qkvz, ba, norm_g, *, batch, seq):
    _, m, d = qkvz.shape
    dk = d // GDN_HEADS
    tb = GDN_TIME_BLOCK
    nt = seq // tb
    n_chunks = tb // GDN_CHUNK

    def col(jc):
        return pl.BlockSpec((None, tb, d), lambda b, t: (jc, b * nt + t, 0))

    body = functools.partial(_gdn_core_body, tb=tb, dk=dk)
    return pl.pallas_call(
        body,
        grid=(batch, nt),
        in_specs=[col(0), col(1), col(2), col(3),
                  pl.BlockSpec((tb, LANES), lambda b, t: (b * nt + t, 0)),
                  pl.BlockSpec((1, dk), lambda b, t: (0, 0))],
        out_specs=pl.BlockSpec((tb, d), lambda b, t: (b * nt + t, 0)),
        out_shape=jax.ShapeDtypeStruct((m, d), bf16),
        scratch_shapes=[
            pltpu.VMEM((GDN_HEADS, dk, dk), f32),
            pltpu.VMEM((GDN_HEADS, tb, dk), f32),
            pltpu.VMEM((GDN_HEADS, tb, dk), bf16),
            pltpu.VMEM((GDN_HEADS, tb, dk), bf16),
            pltpu.VMEM((GDN_HEADS, tb, dk), bf16),
            pltpu.VMEM((GDN_HEADS, tb, GDN_CHUNK), bf16),
            pltpu.VMEM((GDN_HEADS, n_chunks * SUBLANES, dk), f32),
        ],
        compiler_params=_params("arbitrary", "arbitrary"),
        name="gdn_core",
    )(qkvz, qkvz, qkvz, qkvz, ba, norm_g)


def _out_proj_body(x_ref, o_ref, w_ref, g_ref, out_ref):
    out_ref[...] = x_ref[...] + g_ref[...] * _dot(o_ref[...], w_ref[...])


def _out_proj(x, o, w, mod_l, gate_idx, *, seq):
    m, d = x.shape
    tm = ROW_TILE
    tps = seq // tm
    return pl.pallas_call(
        _out_proj_body,
        grid=(m // tm,),
        in_specs=[
            pl.BlockSpec((tm, d), lambda i: (i, 0)),
            pl.BlockSpec((tm, d), lambda i: (i, 0)),
            pl.BlockSpec((d, d), lambda i: (0, 0)),
            _mod_spec(gate_idx, tps, d),
        ],
        out_specs=pl.BlockSpec((tm, d), lambda i: (i, 0)),
        out_shape=jax.ShapeDtypeStruct((m, d), f32),
        compiler_params=_params("arbitrary"),
        name="out_proj",
    )(x, o, w, mod_l)


def _dsw_in_body(x_ref, ng_ref, sh_ref, sc_ref, w_ref, gn_ref, cos_ref, sin_ref,
                 out_ref, h_scr, ya_scr, yb_scr, slab_scr, *, dh, tm, dil):
    n_tiles = w_ref.shape[0]
    y_scr = (ya_scr, yb_scr)
    n = tm // dil

    def write_head(which, h, val):
        sl = slice(h * dh, (h + 1) * dh)
        if dil == 1:
            out_ref[0, which, :, sl] = val.astype(bf16)
        else:
            slab_scr[h] = val
            for r in range(dil):
                out_ref[r, which, :, sl] = slab_scr[h, pl.ds(r, n, stride=dil), :].astype(bf16)

    def epilogue(which):
        y_ref = y_scr[which % 2]
        if which == 2:
            for h in range(DSW_HEADS):
                write_head(which, h, y_ref[:, h * dh:(h + 1) * dh])
            return
        cos_t = cos_ref[...]
        sin_t = sin_ref[...]
        for h in range(DSW_HEADS):
            sl = slice(h * dh, (h + 1) * dh)
            yh = y_ref[:, sl]
            ms = jnp.mean(yh * yh, axis=-1, keepdims=True)
            yn = yh * lax.rsqrt(ms + EPS) * gn_ref[which, :, sl]
            write_head(which, h, yn * cos_t + pltpu.roll(yn, dh // 2, 1) * sin_t)

    def matmul(which):
        y_scr[which % 2][...] = _dot(h_scr[...], w_ref[which])

    j = pl.program_id(1)
    for step in range(n_tiles + 1):
        @pl.when(j == step)
        def _(step=step):
            if step == 0:
                h = _norm_modulate(x_ref[...], ng_ref[...], sh_ref[...], sc_ref[...])
                h_scr[...] = h.astype(bf16)
            if step < n_tiles:
                matmul(step)
            if step > 0:
                epilogue(step - 1)


def _dsw_in_proj(x, mod_l, norm_g, w, gn, cos_t, sin_t, dilation, *, batch, seq):
    m, d = x.shape
    tm = ROW_TILE
    tps = seq // tm
    n_tiles, _, tn = w.shape
    dh = d // DSW_HEADS
    body = functools.partial(_dsw_in_body, dh=dh, tm=tm, dil=dilation)
    rope_spec = pl.BlockSpec((tm, dh), lambda i, j: (i % tps, 0))
    resident = pl.Buffered(1)
    return pl.pallas_call(
        body,
        grid=(m // tm, n_tiles + 1),
        in_specs=[
            pl.BlockSpec((tm, d), lambda i, j: (i, 0)),
            pl.BlockSpec((1, d), lambda i, j: (0, 0)),
            _mod_spec(0, tps, d),
            _mod_spec(1, tps, d),
            pl.BlockSpec((n_tiles, d, tn), lambda i, j: (0, 0, 0), pipeline_mode=resident),
            pl.BlockSpec((n_tiles, 1, tn), lambda i, j: (0, 0, 0), pipeline_mode=resident),
            rope_spec, rope_spec,
        ],
        out_specs=pl.BlockSpec((None, dilation, n_tiles, tm // dilation, tn),
                               lambda i, j: (i // tps, 0, 0, i % tps, 0)),
        out_shape=jax.ShapeDtypeStruct((batch, dilation, n_tiles, seq // dilation, d), bf16),
        scratch_shapes=[pltpu.VMEM((tm, d), bf16), pltpu.VMEM((tm, tn), f32), pltpu.VMEM((tm, tn), f32),
                        pltpu.VMEM((DSW_HEADS, tm, dh), f32)],
        compiler_params=_params("arbitrary", "arbitrary"),
        name=f"dsw_in_proj_d{dilation}",
    )(x, norm_g, mod_l, mod_l, w, gn, cos_t, sin_t)


def _dsw_attn_body(q_ref, kp_ref, kc_ref, vp_ref, vc_ref, o_ref, lse_ref, *, dh, span):
    blk = pl.program_id(2)
    p_len = DSW_BLOCK
    qi = lax.broadcasted_iota(jnp.int32, (p_len, 2 * p_len), 0)
    ki = lax.broadcasted_iota(jnp.int32, (p_len, 2 * p_len), 1)
    rel = qi + p_len - ki
    valid = (rel >= 0) & (rel <= span) & ((blk > 0) | (ki >= p_len))
    lane = lax.broadcasted_iota(jnp.int32, (p_len, LANES), 1)
    lse_tile = jnp.zeros((p_len, LANES), f32)
    hs = range(DSW_HEADS)
    scores = []
    for h in hs:
        sl = slice(h * dh, (h + 1) * dh)
        kcat = jnp.concatenate([kp_ref[:, sl], kc_ref[:, sl]], axis=0)
        scores.append(_dot_nt(q_ref[:, sl], kcat))
    for h in hs:
        sl = slice(h * dh, (h + 1) * dh)
        vcat = jnp.concatenate([vp_ref[:, sl], vc_ref[:, sl]], axis=0)
        s = jnp.where(valid, scores[h], -jnp.inf)
        mx = jnp.max(s, axis=-1, keepdims=True)
        p = jnp.exp(s - mx)
        l = jnp.sum(p, axis=-1, keepdims=True)
        o = _dot(p.astype(bf16), vcat) * (1.0 / l)
        o_ref[:, sl] = o.astype(bf16)
        lse_tile = jnp.where(lane == h, mx + jnp.log(l), lse_tile)
    lse_ref[...] = lse_tile


def _dsw_attention(qkv, window, dilation, *, d):
    batch, _, _, ts, _ = qkv.shape
    dh = d // DSW_HEADS
    nb = ts // DSW_BLOCK
    span = window // dilation

    def in_spec(which, prev):
        def imap(b, r, blk):
            return (b, r, which, jnp.maximum(blk - 1, 0) if prev else blk, 0)
        return pl.BlockSpec((None, None, None, DSW_BLOCK, d), imap)

    body = functools.partial(_dsw_attn_body, dh=dh, span=span)
    return pl.pallas_call(
        body,
        grid=(batch, dilation, nb),
        in_specs=[in_spec(0, False), in_spec(1, True), in_spec(1, False),
                  in_spec(2, True), in_spec(2, False)],
        out_specs=[
            pl.BlockSpec((None, None, DSW_BLOCK, d), lambda b, r, blk: (b, r, blk, 0)),
            pl.BlockSpec((None, None, DSW_BLOCK, LANES), lambda b, r, blk: (b, r, blk, 0)),
        ],
        out_shape=[
            jax.ShapeDtypeStruct((batch, dilation, ts, d), bf16),
            jax.ShapeDtypeStruct((batch, dilation, ts, LANES), f32),
        ],
        compiler_params=_params("arbitrary", "arbitrary", "arbitrary"),
        name=f"dsw_attn_d{dilation}",
    )(qkv, qkv, qkv, qkv, qkv)


def _dsw_out_body(x_ref, o0_ref, o1_ref, o2_ref, l0_ref, l1_ref, l2_ref, w_ref, g_ref,
                  out_ref, lse_scr, nat_scr, om_scr, *, dh, tm):
    o_refs = (o0_ref, o1_ref, o2_ref)
    l_refs = (l0_ref, l1_ref, l2_ref)
    groups = range(len(DSW_PATTERNS))

    def token_rows(g, r):
        dil = DSW_PATTERNS[g][1]
        return pl.ds(r, tm // dil, stride=dil)

    for g in groups:
        for r in range(DSW_PATTERNS[g][1]):
            lse_scr[g, token_rows(g, r), :] = l_refs[g][r]
    lse = [lse_scr[g] for g in groups]
    mx = jnp.maximum(jnp.maximum(lse[0], lse[1]), lse[2])
    e = [jnp.exp(lse[g] - mx) for g in groups]
    inv = 1.0 / (e[0] + e[1] + e[2])
    wts = [e[g] * inv for g in groups]
    for h in range(DSW_HEADS):
        sl = slice(h * dh, (h + 1) * dh)
        om = None
        for g in groups:
            for r in range(DSW_PATTERNS[g][1]):
                nat_scr[g, token_rows(g, r), :] = o_refs[g][r, :, sl].astype(f32)
            term = wts[g][:, h:h + 1] * nat_scr[g]
            om = term if om is None else om + term
        om_scr[:, sl] = om.astype(bf16)
    out_ref[...] = x_ref[...] + g_ref[...] * _dot(om_scr[...], w_ref[...])


def _dsw_out_proj(x, outs, lses, w, mod_l, *, seq):
    m, d = x.shape
    tm = DSW_OUT_ROW_TILE
    tps = seq // tm
    dh = d // DSW_HEADS
    row = pl.BlockSpec((tm, d), lambda i: (i, 0))

    def stream_spec(g, width):
        dil = DSW_PATTERNS[g][1]
        return pl.BlockSpec((None, dil, tm // dil, width), lambda i: (i // tps, 0, i % tps, 0))

    groups = range(len(DSW_PATTERNS))
    body = functools.partial(_dsw_out_body, dh=dh, tm=tm)
    return pl.pallas_call(
        body,
        grid=(m // tm,),
        in_specs=[row] + [stream_spec(g, d) for g in groups] + [stream_spec(g, LANES) for g in groups]
                 + [pl.BlockSpec((d, d), lambda i: (0, 0)), _mod_spec(2, tps, d)],
        out_specs=row,
        out_shape=jax.ShapeDtypeStruct((m, d), f32),
        scratch_shapes=[pltpu.VMEM((len(DSW_PATTERNS), tm, LANES), f32),
                        pltpu.VMEM((len(DSW_PATTERNS), tm, dh), f32),
                        pltpu.VMEM((tm, d), bf16)],
        compiler_params=_params("arbitrary"),
        name="dsw_out_proj",
    )(x, *outs, *lses, w, mod_l)


def _ffn_body(x_ref, ng_ref, sh_ref, sc_ref, g_ref, wgu_ref, wd_ref, out_ref, h_scr, *, th):
    hidden = wd_ref.shape[0]
    h = _norm_modulate(x_ref[...], ng_ref[...], sh_ref[...], sc_ref[...])
    h_scr[...] = h.astype(bf16)
    hb = h_scr[...]
    acc = None
    act_prev = None
    for t in range(hidden // th + 1):
        if t * th < hidden:
            gate = _dot(hb, wgu_ref[:, t * th:(t + 1) * th])
            up = _dot(hb, wgu_ref[:, hidden + t * th:hidden + (t + 1) * th])
        if act_prev is not None:
            part = _dot(act_prev, wd_ref[(t - 1) * th:t * th, :])
            acc = part if acc is None else acc + part
        if t * th < hidden:
            act_prev = (_silu(gate) * up).astype(bf16)
    out_ref[...] = x_ref[...] + g_ref[...] * acc


def _ffn(x, mod_l, norm_g, w_gate_up, w_down, *, seq):
    m, d = x.shape
    tm = ROW_TILE
    tps = seq // tm
    hidden = w_down.shape[0]
    resident = pl.Buffered(1)
    body = functools.partial(_ffn_body, th=FFN_HIDDEN_TILE)
    return pl.pallas_call(
        body,
        grid=(m // tm,),
        in_specs=[
            pl.BlockSpec((tm, d), lambda i: (i, 0)),
            pl.BlockSpec((1, d), lambda i: (0, 0)),
            _mod_spec(3, tps, d),
            _mod_spec(4, tps, d),
            _mod_spec(5, tps, d),
            pl.BlockSpec((d, 2 * hidden), lambda i: (0, 0), pipeline_mode=resident),
            pl.BlockSpec((hidden, d), lambda i: (0, 0), pipeline_mode=resident),
        ],
        out_specs=pl.BlockSpec((tm, d), lambda i: (i, 0)),
        out_shape=jax.ShapeDtypeStruct((m, d), f32),
        scratch_shapes=[pltpu.VMEM((tm, d), bf16)],
        compiler_params=_params("arbitrary"),
        name="ffn",
    )(x, norm_g, mod_l, mod_l, mod_l, w_gate_up, w_down)


def _rope_tables(seq, dh):
    rope_dim = dh // 4
    half = rope_dim // 2
    inv = jnp.exp(-math.log(ROPE_THETA) * (2.0 * jnp.arange(half, dtype=f32) / rope_dim))
    ang = jnp.arange(seq, dtype=f32)[:, None] * inv[None, :]
    cos = jnp.cos(ang)
    sin = jnp.sin(ang)
    ones = jnp.ones((seq, dh // 2 - half), f32)
    zeros = jnp.zeros((seq, dh // 2 - half), f32)
    cos_t = jnp.concatenate([cos, ones, cos, ones], axis=-1)
    sin_t = jnp.concatenate([-sin, zeros, sin, zeros], axis=-1)
    return cos_t, sin_t


def _rope_lane_order(t, dh):
    half = (dh // 4) // 2
    return jnp.concatenate([t[..., :half], t[..., 2 * half:dh // 2 + half],
                            t[..., half:2 * half], t[..., dh // 2 + half:]], axis=-1)


def kernel(x, c, mod_w, mod_b, mix_norm_g, ffn_norm_g, gdn_w_in, gdn_conv_w, gdn_A_log,
           gdn_dt_bias, gdn_norm_g, gdn_w_out, dsw_w_in, dsw_q_norm_g, dsw_k_norm_g,
           dsw_w_out, ffn_w_gate_up, ffn_w_down):
    batch, seq, d = x.shape
    depth = mod_w.shape[0]
    m = batch * seq
    heads = GDN_HEADS
    dh = d // DSW_HEADS
    assert seq % ROW_TILE == 0 and seq % GDN_TIME_BLOCK == 0 and d == heads * LANES

    mod_rows = 16
    c_pad = jnp.zeros((mod_rows, d), f32).at[:batch].set(c)
    mod = _modulation(c_pad, mod_w, mod_b)[:, :batch]
    mod = mod.reshape(depth, batch, 6, 1, d)

    cos_t, sin_t = _rope_tables(seq, dh)
    xf = x.reshape(m, d)

    for layer in range(depth):
        mod_l = mod[layer]
        j = layer // N_MIXERS
        norm_g = mix_norm_g[layer].reshape(1, d)
        if layer % N_MIXERS == 0:
            n_conv = 3 * d
            w_in = gdn_w_in[j]
            w4 = w_in[:, :4 * d].reshape(d, 4, d).transpose(1, 0, 2).astype(bf16)
            wba = jnp.zeros((d, LANES), f32).at[:, :2 * heads].set(w_in[:, 4 * d:]).astype(bf16)
            cw = jnp.zeros((GDN_CONV, 4 * d), f32).at[:, :n_conv].set(gdn_conv_w[j])
            cw = cw.reshape(GDN_CONV, 4, d).transpose(1, 0, 2)
            gp = jnp.zeros((SUBLANES, LANES), f32)
            gp = gp.at[0, heads:2 * heads].set(gdn_A_log[j]).at[1, heads:2 * heads].set(gdn_dt_bias[j])
            qkvz, ba = _gdn_in_proj(xf, mod_l, norm_g, w4, wba, cw, gp, seq=seq)
            o = _gdn_core(qkvz, ba, gdn_norm_g[j].reshape(1, -1), batch=batch, seq=seq)
            xf = _out_proj(xf, o, gdn_w_out[j].astype(bf16), mod_l, 2, seq=seq)
        else:
            n_groups = len(DSW_PATTERNS)
            w5 = dsw_w_in[j].reshape(d, n_groups, 3, DSW_HEADS, dh)
            w5 = jnp.concatenate([_rope_lane_order(w5[:, :, :2], dh), w5[:, :, 2:]], axis=2)
            w_in = w5.reshape(d, 3 * n_groups, d).transpose(1, 0, 2).astype(bf16)
            gains = []
            for gi in range(n_groups):
                q_gain = _rope_lane_order(dsw_q_norm_g[j, gi], dh) * (dh ** -0.5)
                gains.append(jnp.tile(q_gain, DSW_HEADS))
                gains.append(jnp.tile(_rope_lane_order(dsw_k_norm_g[j, gi], dh), DSW_HEADS))
                gains.append(jnp.ones((d,), f32))
            gn = jnp.stack(gains).reshape(3 * n_groups, 1, d)
            outs, lses = [], []
            for gi, (window, dilation) in enumerate(DSW_PATTERNS):
                qkv = _dsw_in_proj(xf, mod_l, norm_g, w_in[3 * gi:3 * gi + 3], gn[3 * gi:3 * gi + 3],
                                   cos_t, sin_t, dilation, batch=batch, seq=seq)
                o_g, lse_g = _dsw_attention(qkv, window, dilation, d=d)
                outs.append(o_g)
                lses.append(lse_g)
            xf = _dsw_out_proj(xf, outs, lses, dsw_w_out[j].astype(bf16), mod_l, seq=seq)
        xf = _ffn(xf, mod_l, ffn_norm_g[layer].reshape(1, d), ffn_w_gate_up[layer].astype(bf16),
                  ffn_w_down[layer].astype(bf16), seq=seq)
    return xf.reshape(batch, seq, d)
```

```python
import functools
import math

import jax
import jax.numpy as jnp
from jax import lax
from jax.experimental import pallas as pl
from jax.experimental.pallas import tpu as pltpu

EPS = 1e-6
N_MIXERS = 2
GDN_HEADS = 8
GDN_CONV = 4
GDN_CHUNK = 64
GDN_SOLVE_BLOCK = 16
GDN_SCAN_CHUNKS = 2
GDN_PREP_CHUNKS = 4
DSW_PATTERNS = ((128, 1), (512, 4), (2048, 16))
DSW_HEADS = 8
DSW_BLOCK = 128
DSW_BLOCKS_PER_STEP = 4
ROPE_THETA = 500000.0

LANES = 128
SUBLANES = 8
VMEM_LIMIT_BYTES = 56 * 1024 * 1024

ROW_TILE = 1024
DSW_OUT_ROW_TILE = 512
GDN_TIME_BLOCK = 512
FFN_HIDDEN_TILE = 256

f32 = jnp.float32
bf16 = jnp.bfloat16


def _params(*sem):
    return pltpu.CompilerParams(dimension_semantics=sem, vmem_limit_bytes=VMEM_LIMIT_BYTES)


def _dot(a, b):
    return jnp.dot(a, b, preferred_element_type=f32)


def _dot_nt(a, b):
    return lax.dot_general(a, b, (((1,), (1,)), ((), ())), preferred_element_type=f32)


def _dot_tn(a, b):
    return lax.dot_general(a, b, (((0,), (0,)), ((), ())), preferred_element_type=f32)


def _dot_round(a, b):
    return _dot(a.astype(bf16), b.astype(bf16))


def _silu(x):
    return x * (0.5 + 0.5 * jnp.tanh(0.5 * x))


def _softplus(x):
    return jnp.maximum(x, 0.0) + jnp.log1p(jnp.exp(-jnp.abs(x)))


def _norm_modulate(x, gain, shift, scale):
    ms = jnp.mean(x * x, axis=-1, keepdims=True)
    y = x * lax.rsqrt(ms + EPS) * gain
    return y * (1.0 + scale) + shift


def _mod_body(c_ref, w_ref, b_ref, o_ref):
    c = c_ref[...]
    cond = _silu(c).astype(bf16)
    o_ref[...] = _dot(cond, w_ref[...].astype(bf16)) + b_ref[...]


def _modulation(c_pad, mod_w, mod_b):
    depth, d, n = mod_w.shape
    rows = c_pad.shape[0]
    tn = 1536
    return pl.pallas_call(
        _mod_body,
        grid=(depth, n // tn),
        in_specs=[
            pl.BlockSpec((rows, d), lambda l, j: (0, 0)),
            pl.BlockSpec((None, d, tn), lambda l, j: (l, 0, j)),
            pl.BlockSpec((None, 1, tn), lambda l, j: (l, 0, j)),
        ],
        out_specs=pl.BlockSpec((None, rows, tn), lambda l, j: (l, 0, j)),
        out_shape=jax.ShapeDtypeStruct((depth, rows, n), f32),
        compiler_params=_params("arbitrary", "arbitrary"),
        name="modulation",
    )(c_pad, mod_w, mod_b.reshape(depth, 1, n))


def _mod_spec(idx, tiles_per_seq, d):
    return pl.BlockSpec((None, None, 1, d), lambda i, *_: (i // tiles_per_seq, idx, 0, 0))


def _gdn_in_body(x_ref, ng_ref, sh_ref, sc_ref, w_ref, wba_ref, cw_ref, gp_ref,
                 out_ref, ba_ref, h_scr, ya_scr, yb_scr, tail_scr, *, tiles_per_seq, tm, dk):
    i = pl.program_id(0)
    n_tiles = w_ref.shape[0]
    y_scr = (ya_scr, yb_scr)
    first = (i % tiles_per_seq) == 0
    n_conv_tiles = n_tiles - 1

    @pl.when(i == 0)
    def _():
        tail_scr[...] = jnp.zeros(tail_scr.shape, f32)

    def epilogue(tile):
        y_ref = y_scr[tile % 2]
        if tile == n_conv_tiles:
            out_ref[tile] = y_ref[SUBLANES:, :].astype(bf16)
            return
        y_ref[0:SUBLANES, :] = jnp.where(first, 0.0, tail_scr[tile])
        tail_scr[tile] = y_ref[tm:, :]
        post = dk ** -0.5 if tile == 0 else 1.0
        for h in range(GDN_HEADS):
            sl = slice(h * dk, (h + 1) * dk)
            acc = cw_ref[tile, GDN_CONV - 1:GDN_CONV, sl] * y_ref[SUBLANES:, sl]
            for kk in range(GDN_CONV - 1):
                start = SUBLANES - (GDN_CONV - 1) + kk
                acc = acc + cw_ref[tile, kk:kk + 1, sl] * y_ref[pl.ds(start, tm), sl]
            a = _silu(acc)
            if tile < 2:
                ss = jnp.sum(a * a, axis=-1, keepdims=True)
                a = a * (lax.rsqrt(ss + EPS) * post)
            out_ref[tile, :, sl] = a.astype(bf16)

    def matmul(tile):
        y_scr[tile % 2][SUBLANES:, :] = _dot(h_scr[...], w_ref[tile])

    h = _norm_modulate(x_ref[...], ng_ref[...], sh_ref[...], sc_ref[...])
    hb = h.astype(bf16)
    h_scr[...] = hb
    matmul(0)
    ba = _dot(hb, wba_ref[...])
    lane = lax.broadcasted_iota(jnp.int32, ba.shape, 1)
    beta = jax.nn.sigmoid(ba)
    g = -jnp.exp(gp_ref[0:1, :]) * _softplus(ba + gp_ref[1:2, :])
    ba_ref[...] = jnp.where(lane < GDN_HEADS, beta, g)
    for tile in range(n_tiles):
        if tile + 1 < n_tiles:
            matmul(tile + 1)
        epilogue(tile)


def _gdn_in_proj(x, mod_l, norm_g, w4, wba, cw, gp, *, seq):
    m, d = x.shape
    tm = ROW_TILE
    tps = seq // tm
    n_tiles, _, tn = w4.shape
    dk = tn // GDN_HEADS
    body = functools.partial(_gdn_in_body, tiles_per_seq=tps, tm=tm, dk=dk)
    resident = pl.Buffered(1)
    return pl.pallas_call(
        body,
        grid=(m // tm,),
        in_specs=[
            pl.BlockSpec((tm, d), lambda i: (i, 0)),
            pl.BlockSpec((1, d), lambda i: (0, 0)),
            _mod_spec(0, tps, d),
            _mod_spec(1, tps, d),
            pl.BlockSpec((n_tiles, d, tn), lambda i: (0, 0, 0), pipeline_mode=resident),
            pl.BlockSpec((d, LANES), lambda i: (0, 0)),
            pl.BlockSpec((n_tiles, GDN_CONV, tn), lambda i: (0, 0, 0)),
            pl.BlockSpec((SUBLANES, LANES), lambda i: (0, 0)),
        ],
        out_specs=[
            pl.BlockSpec((n_tiles, tm, tn), lambda i: (0, i, 0)),
            pl.BlockSpec((tm, LANES), lambda i: (i, 0)),
        ],
        out_shape=[
            jax.ShapeDtypeStruct((n_tiles, m, tn), bf16),
            jax.ShapeDtypeStruct((m, LANES), f32),
        ],
        scratch_shapes=[
            pltpu.VMEM((tm, d), bf16),
            pltpu.VMEM((tm + SUBLANES, tn), f32),
            pltpu.VMEM((tm + SUBLANES, tn), f32),
            pltpu.VMEM((n_tiles - 1, SUBLANES, tn), f32),
        ],
        compiler_params=_params("arbitrary"),
        name="gdn_in_proj",
    )(x, norm_g, mod_l, mod_l, w4, wba, cw, gp)


def _gdn_core_body(q_ref, k_ref, v_ref, z_ref, ba_ref, ng_ref, o_ref,
                   s_scr, u_scr, w_scr, qd_scr, kt_scr, a_scr, gl_scr, *, tb, dk):
    c_len = GDN_CHUNK
    n_chunks = tb // c_len
    heads = GDN_HEADS
    hs = range(heads)

    @pl.when(pl.program_id(1) == 0)
    def _():
        s_scr[...] = jnp.zeros(s_scr.shape, f32)

    row = lax.broadcasted_iota(jnp.int32, (c_len, c_len), 0)
    col = lax.broadcasted_iota(jnp.int32, (c_len, c_len), 1)
    causal = row >= col
    strict = row > col
    tril_f = causal.astype(f32)
    eye_f = (row == col).astype(f32)
    diag_blk = (row // GDN_SOLVE_BLOCK) == (col // GDN_SOLVE_BLOCK)

    def prep(cp, carry):
        x, rhs, decay, where = [], [], [], []
        for ci in range(GDN_PREP_CHUNKS):
            c = cp * GDN_PREP_CHUNKS + ci
            rows = pl.ds(pl.multiple_of(c * c_len, c_len), c_len)
            g8 = pl.multiple_of(c * SUBLANES, SUBLANES)
            ba = ba_ref[rows, :]
            gc_all = jnp.dot(tril_f, ba, precision=lax.Precision.HIGHEST,
                             preferred_element_type=f32)
            gc_t = gc_all.T
            for h in hs:
                sl = slice(h * dk, (h + 1) * dk)
                q = q_ref[rows, sl]
                k = k_ref[rows, sl]
                beta = ba[:, h:h + 1]
                gc = gc_all[:, heads + h:heads + h + 1]
                gr = gc_t[heads + h:heads + h + 1, :]
                g_last = gc_all[c_len - 1:c_len, heads + h:heads + h + 1]
                dec = jnp.where(causal, jnp.exp(jnp.minimum(gc - gr, 0.0)), 0.0)
                eg = jnp.exp(gc)
                kf = k.astype(f32)
                kb = kf * beta
                kk = _dot_nt(kb.astype(bf16), k)
                x.append(jnp.where(strict, -(kk * dec), 0.0))
                rhs.append(jnp.concatenate([v_ref[rows, sl].astype(f32) * beta, kb * eg], axis=-1))
                decay.append(dec)
                where.append((h, rows, sl))
                qd_scr[h, rows, :] = (q.astype(f32) * eg).astype(bf16)
                kt_scr[h, rows, :] = (kf * jnp.exp(g_last - gc)).astype(bf16)
                gl_scr[h, pl.ds(g8, SUBLANES), :] = jnp.broadcast_to(jnp.exp(g_last), (SUBLANES, dk))
        items = range(len(x))
        for it in items:
            h, rows, sl = where[it]
            a_in = jnp.where(causal, _dot_nt(q_ref[rows, sl], k_ref[rows, sl]) * decay[it], 0.0)
            a_scr[h, rows, :] = a_in.astype(bf16)
        x_d = [jnp.where(diag_blk, x[it], 0.0) for it in items]
        d_inv = [eye_f + x_d[it] for it in items]
        xp = x_d
        for _ in range(int(math.log2(GDN_SOLVE_BLOCK)) - 1):
            xp = [_dot_round(xp[it], xp[it]) for it in items]
            d_inv = [d_inv[it] + _dot_round(d_inv[it], xp[it]) for it in items]
        mn = [_dot_round(d_inv[it], x[it] - x_d[it]) for it in items]
        o_inv = [eye_f + mn[it] for it in items]
        mp = mn
        for _ in range(int(math.log2(c_len // GDN_SOLVE_BLOCK)) - 1):
            mp = [_dot_round(mp[it], mp[it]) for it in items]
            o_inv = [o_inv[it] + _dot_round(o_inv[it], mp[it]) for it in items]
        t_inv = [_dot_round(o_inv[it], d_inv[it]) for it in items]
        sol = [_dot_round(t_inv[it], rhs[it]) for it in items]
        for it in items:
            h, rows, _ = where[it]
            u_scr[h, rows, :] = sol[it][:, :dk]
            w_scr[h, rows, :] = sol[it][:, dk:].astype(bf16)
        return carry

    lax.fori_loop(0, n_chunks // GDN_PREP_CHUNKS, prep, 0)

    ng = ng_ref[...]

    def scan_chunk(c, s):
        rows = pl.ds(pl.multiple_of(c * c_len, c_len), c_len)
        g8 = pl.multiple_of(c * SUBLANES, SUBLANES)
        sb = [s[h].astype(bf16) for h in hs]
        ws = [_dot(w_scr[h, rows, :], sb[h]) for h in hs]
        qs = [_dot(qd_scr[h, rows, :], sb[h]) for h in hs]
        v_new = [(u_scr[h, rows, :] - ws[h]).astype(bf16) for h in hs]
        s_new = [s[h] * gl_scr[h, pl.ds(g8, 1), :] + _dot_tn(kt_scr[h, rows, :], v_new[h]) for h in hs]
        o = [qs[h] + _dot(a_scr[h, rows, :], v_new[h]) for h in hs]
        for h in hs:
            sl = slice(h * dk, (h + 1) * dk)
            z = z_ref[rows, sl].astype(f32)
            ms = jnp.mean(o[h] * o[h], axis=-1, keepdims=True)
            o_ref[rows, sl] = (o[h] * lax.rsqrt(ms + EPS) * ng * _silu(z)).astype(bf16)
        return s_new

    def scan(cp, carry):
        s = [s_scr[h] for h in hs]
        for ci in range(GDN_SCAN_CHUNKS):
            s = scan_chunk(cp * GDN_SCAN_CHUNKS + ci, s)
        for h in hs:
            s_scr[h] = s[h]
        return carry

    lax.fori_loop(0, n_chunks // GDN_SCAN_CHUNKS, scan, 0)


def _gdn_core(qkvz, ba, norm_g, *, batch, seq):
    _, m, d = qkvz.shape
    dk = d // GDN_HEADS
    tb = GDN_TIME_BLOCK
    nt = seq // tb
    n_chunks = tb // GDN_CHUNK

    def col(jc):
        return pl.BlockSpec((None, tb, d), lambda b, t: (jc, b * nt + t, 0))

    body = functools.partial(_gdn_core_body, tb=tb, dk=dk)
    return pl.pallas_call(
        body,
        grid=(batch, nt),
        in_specs=[col(0), col(1), col(2), col(3),
                  pl.BlockSpec((tb, LANES), lambda b, t: (b * nt + t, 0)),
                  pl.BlockSpec((1, dk), lambda b, t: (0, 0))],
        out_specs=pl.BlockSpec((tb, d), lambda b, t: (b * nt + t, 0)),
        out_shape=jax.ShapeDtypeStruct((m, d), bf16),
        scratch_shapes=[
            pltpu.VMEM((GDN_HEADS, dk, dk), f32),
            pltpu.VMEM((GDN_HEADS, tb, dk), f32),
            pltpu.VMEM((GDN_HEADS, tb, dk), bf16),
            pltpu.VMEM((GDN_HEADS, tb, dk), bf16),
            pltpu.VMEM((GDN_HEADS, tb, dk), bf16),
            pltpu.VMEM((GDN_HEADS, tb, GDN_CHUNK), bf16),
            pltpu.VMEM((GDN_HEADS, n_chunks * SUBLANES, dk), f32),
        ],
        compiler_params=_params("arbitrary", "arbitrary"),
        name="gdn_core",
    )(qkvz, qkvz, qkvz, qkvz, ba, norm_g)


def _out_proj_body(x_ref, o_ref, w_ref, g_ref, out_ref):
    out_ref[...] = x_ref[...] + g_ref[...] * _dot(o_ref[...], w_ref[...])


def _out_proj(x, o, w, mod_l, gate_idx, *, seq):
    m, d = x.shape
    tm = ROW_TILE
    tps = seq // tm
    return pl.pallas_call(
        _out_proj_body,
        grid=(m // tm,),
        in_specs=[
            pl.BlockSpec((tm, d), lambda i: (i, 0)),
            pl.BlockSpec((tm, d), lambda i: (i, 0)),
            pl.BlockSpec((d, d), lambda i: (0, 0)),
            _mod_spec(gate_idx, tps, d),
        ],
        out_specs=pl.BlockSpec((tm, d), lambda i: (i, 0)),
        out_shape=jax.ShapeDtypeStruct((m, d), f32),
        compiler_params=_params("arbitrary"),
        name="out_proj",
    )(x, o, w, mod_l)


def _dsw_in_body(x_ref, ng_ref, sh_ref, sc_ref, w_ref, gn_ref, cos_ref, sin_ref,
                 o0_ref, o1_ref, o2_ref, h_scr, ya_scr, yb_scr, slab_scr, *, dh, tm, n_tiles):
    j = pl.program_id(1)
    y_scr = (ya_scr, yb_scr)
    out_refs = (o0_ref, o1_ref, o2_ref)

    def write_head(g, h, val):
        dil = DSW_PATTERNS[g][1]
        n = tm // dil
        sl = slice(h * dh, (h + 1) * dh)
        if dil == 1:
            out_refs[g][0, :, sl] = val.astype(bf16)
        else:
            slab_scr[h] = val
            for r in range(dil):
                out_refs[g][r, :, sl] = slab_scr[h, pl.ds(r, n, stride=dil), :].astype(bf16)

    def epilogue(tile):
        g, which = divmod(tile, 3)
        y_ref = y_scr[tile % 2]
        if which == 2:
            for h in range(DSW_HEADS):
                write_head(g, h, y_ref[:, h * dh:(h + 1) * dh])
            return
        cos_t = cos_ref[...]
        sin_t = sin_ref[...]
        for h in range(DSW_HEADS):
            sl = slice(h * dh, (h + 1) * dh)
            yh = y_ref[:, sl]
            ms = jnp.mean(yh * yh, axis=-1, keepdims=True)
            yn = yh * lax.rsqrt(ms + EPS) * gn_ref[:, sl]
            write_head(g, h, yn * cos_t + pltpu.roll(yn, dh // 2, 1) * sin_t)

    for step in range(n_tiles + 1):
        @pl.when(j == step)
        def _(step=step):
            if step == 0:
                h = _norm_modulate(x_ref[...], ng_ref[...], sh_ref[...], sc_ref[...])
                h_scr[...] = h.astype(bf16)
            if step < n_tiles:
                y_scr[step % 2][...] = _dot(h_scr[...], w_ref[...])
            if step > 0:
                epilogue(step - 1)


def _dsw_in_proj(x, mod_l, norm_g, w, gn, cos_t, sin_t, *, batch, seq):
    m, d = x.shape
    tm = ROW_TILE
    tps = seq // tm
    n_tiles, _, tn = w.shape
    dh = d // DSW_HEADS
    body = functools.partial(_dsw_in_body, dh=dh, tm=tm, n_tiles=n_tiles)
    rope_spec = pl.BlockSpec((tm, dh), lambda i, j: (i % tps, 0))

    def out_spec(g):
        dil = DSW_PATTERNS[g][1]
        return pl.BlockSpec((None, dil, None, tm // dil, tn),
                            lambda i, j: (i // tps, 0, jnp.clip(j - 1 - 3 * g, 0, 2), i % tps, 0))

    groups = range(len(DSW_PATTERNS))
    return pl.pallas_call(
        body,
        grid=(m // tm, n_tiles + 1),
        in_specs=[
            pl.BlockSpec((tm, d), lambda i, j: (i, 0)),
            pl.BlockSpec((1, d), lambda i, j: (0, 0)),
            _mod_spec(0, tps, d),
            _mod_spec(1, tps, d),
            pl.BlockSpec((None, d, tn), lambda i, j: (jnp.minimum(j, n_tiles - 1), 0, 0)),
            pl.BlockSpec((None, 1, tn), lambda i, j: (jnp.maximum(j - 1, 0), 0, 0)),
            rope_spec, rope_spec,
        ],
        out_specs=[out_spec(g) for g in groups],
        out_shape=[jax.ShapeDtypeStruct((batch, DSW_PATTERNS[g][1], 3, seq // DSW_PATTERNS[g][1], d), bf16)
                   for g in groups],
        scratch_shapes=[pltpu.VMEM((tm, d), bf16), pltpu.VMEM((tm, tn), f32), pltpu.VMEM((tm, tn), f32),
                        pltpu.VMEM((DSW_HEADS, tm, dh), f32)],
        compiler_params=_params("arbitrary", "arbitrary"),
        name="dsw_in_proj",
    )(x, norm_g, mod_l, mod_l, w, gn, cos_t, sin_t)


def _dsw_attn_body(q_ref, kp_ref, kc_ref, vp_ref, vc_ref, o_ref, lse_ref, *, dh, span, nq):
    first_step = pl.program_id(2) == 0
    p_len = DSW_BLOCK
    qi = lax.broadcasted_iota(jnp.int32, (p_len, 2 * p_len), 0)
    ki = lax.broadcasted_iota(jnp.int32, (p_len, 2 * p_len), 1)
    rel = qi + p_len - ki
    band = (rel >= 0) & (rel <= span)
    lane = lax.broadcasted_iota(jnp.int32, (p_len, LANES), 1)
    hs = range(DSW_HEADS)

    def prev_and_cur(prev_ref, cur_ref, sb, sl):
        cur = cur_ref[sb * p_len:(sb + 1) * p_len, sl]
        prev = prev_ref[:, sl] if sb == 0 else cur_ref[(sb - 1) * p_len:sb * p_len, sl]
        return jnp.concatenate([prev, cur], axis=0)

    def block_scores(sb):
        return [_dot_nt(q_ref[sb * p_len:(sb + 1) * p_len, h * dh:(h + 1) * dh],
                        prev_and_cur(kp_ref, kc_ref, sb, slice(h * dh, (h + 1) * dh))) for h in hs]

    scores = block_scores(0)
    for sb in range(nq):
        nxt = block_scores(sb + 1) if sb + 1 < nq else None
        rows = slice(sb * p_len, (sb + 1) * p_len)
        valid = band & (jnp.logical_not(first_step) | (ki >= p_len)) if sb == 0 else band
        lse_tile = jnp.zeros((p_len, LANES), f32)
        for h in hs:
            sl = slice(h * dh, (h + 1) * dh)
            s = jnp.where(valid, scores[h], -jnp.inf)
            mx = jnp.max(s, axis=-1, keepdims=True)
            p = jnp.exp(s - mx)
            l = jnp.sum(p, axis=-1, keepdims=True)
            o = _dot(p.astype(bf16), prev_and_cur(vp_ref, vc_ref, sb, sl)) * (1.0 / l)
            o_ref[rows, sl] = o.astype(bf16)
            lse_tile = jnp.where(lane == h, mx + jnp.log(l), lse_tile)
        lse_ref[rows, :] = lse_tile
        scores = nxt


def _dsw_attention(qkv, window, dilation, *, d):
    batch, _, _, ts, _ = qkv.shape
    dh = d // DSW_HEADS
    nb = ts // DSW_BLOCK
    nq = min(DSW_BLOCKS_PER_STEP, nb)
    span = window // dilation

    def in_spec(which, prev):
        if prev:
            return pl.BlockSpec((None, None, None, DSW_BLOCK, d),
                                lambda b, r, s: (b, r, which, jnp.maximum(s * nq - 1, 0), 0))
        return pl.BlockSpec((None, None, None, nq * DSW_BLOCK, d), lambda b, r, s: (b, r, which, s, 0))

    body = functools.partial(_dsw_attn_body, dh=dh, span=span, nq=nq)
    return pl.pallas_call(
        body,
        grid=(batch, dilation, nb // nq),
        in_specs=[in_spec(0, False), in_spec(1, True), in_spec(1, False),
                  in_spec(2, True), in_spec(2, False)],
        out_specs=[
            pl.BlockSpec((None, None, nq * DSW_BLOCK, d), lambda b, r, s: (b, r, s, 0)),
            pl.BlockSpec((None, None, nq * DSW_BLOCK, LANES), lambda b, r, s: (b, r, s, 0)),
        ],
        out_shape=[
            jax.ShapeDtypeStruct((batch, dilation, ts, d), bf16),
            jax.ShapeDtypeStruct((batch, dilation, ts, LANES), f32),
        ],
        compiler_params=_params("arbitrary", "arbitrary", "arbitrary"),
        name=f"dsw_attn_d{dilation}",
    )(qkv, qkv, qkv, qkv, qkv)


def _dsw_out_body(x_ref, o0_ref, o1_ref, o2_ref, l0_ref, l1_ref, l2_ref, w_ref, g_ref,
                  out_ref, lse_scr, nat_scr, om_scr, *, dh, tm):
    o_refs = (o0_ref, o1_ref, o2_ref)
    l_refs = (l0_ref, l1_ref, l2_ref)
    groups = range(len(DSW_PATTERNS))

    def token_rows(g, r):
        dil = DSW_PATTERNS[g][1]
        return pl.ds(r, tm // dil, stride=dil)

    for g in groups:
        for r in range(DSW_PATTERNS[g][1]):
            lse_scr[g, token_rows(g, r), :] = l_refs[g][r]
    lse = [lse_scr[g] for g in groups]
    mx = jnp.maximum(jnp.maximum(lse[0], lse[1]), lse[2])
    e = [jnp.exp(lse[g] - mx) for g in groups]
    inv = 1.0 / (e[0] + e[1] + e[2])
    wts = [e[g] * inv for g in groups]
    for h in range(DSW_HEADS):
        sl = slice(h * dh, (h + 1) * dh)
        om = None
        for g in groups:
            for r in range(DSW_PATTERNS[g][1]):
                nat_scr[g, token_rows(g, r), :] = o_refs[g][r, :, sl].astype(f32)
            term = wts[g][:, h:h + 1] * nat_scr[g]
            om = term if om is None else om + term
        om_scr[:, sl] = om.astype(bf16)
    out_ref[...] = x_ref[...] + g_ref[...] * _dot(om_scr[...], w_ref[...])


def _dsw_out_proj(x, outs, lses, w, mod_l, *, seq):
    m, d = x.shape
    tm = DSW_OUT_ROW_TILE
    tps = seq // tm
    dh = d // DSW_HEADS
    row = pl.BlockSpec((tm, d), lambda i: (i, 0))

    def stream_spec(g, width):
        dil = DSW_PATTERNS[g][1]
        return pl.BlockSpec((None, dil, tm // dil, width), lambda i: (i // tps, 0, i % tps, 0))

    groups = range(len(DSW_PATTERNS))
    body = functools.partial(_dsw_out_body, dh=dh, tm=tm)
    return pl.pallas_call(
        body,
        grid=(m // tm,),
        in_specs=[row] + [stream_spec(g, d) for g in groups] + [stream_spec(g, LANES) for g in groups]
                 + [pl.BlockSpec((d, d), lambda i: (0, 0)), _mod_spec(2, tps, d)],
        out_specs=row,
        out_shape=jax.ShapeDtypeStruct((m, d), f32),
        scratch_shapes=[pltpu.VMEM((len(DSW_PATTERNS), tm, LANES), f32),
                        pltpu.VMEM((len(DSW_PATTERNS), tm, dh), f32),
                        pltpu.VMEM((tm, d), bf16)],
        compiler_params=_params("arbitrary"),
        name="dsw_out_proj",
    )(x, *outs, *lses, w, mod_l)


def _ffn_body(x_ref, ng_ref, sh_ref, sc_ref, g_ref, wgu_ref, wd_ref, out_ref, h_scr, *, th):
    hidden = wd_ref.shape[0]
    h = _norm_modulate(x_ref[...], ng_ref[...], sh_ref[...], sc_ref[...])
    h_scr[...] = h.astype(bf16)
    hb = h_scr[...]
    acc = None
    act_prev = None
    for t in range(hidden // th + 1):
        if t * th < hidden:
            gate = _dot(hb, wgu_ref[:, t * th:(t + 1) * th])
            up = _dot(hb, wgu_ref[:, hidden + t * th:hidden + (t + 1) * th])
        if act_prev is not None:
            part = _dot(act_prev, wd_ref[(t - 1) * th:t * th, :])
            acc = part if acc is None else acc + part
        if t * th < hidden:
            act_prev = (_silu(gate) * up).astype(bf16)
    out_ref[...] = x_ref[...] + g_ref[...] * acc


def _ffn(x, mod_l, norm_g, w_gate_up, w_down, *, seq):
    m, d = x.shape
    tm = ROW_TILE
    tps = seq // tm
    hidden = w_down.shape[0]
    resident = pl.Buffered(1)
    body = functools.partial(_ffn_body, th=FFN_HIDDEN_TILE)
    return pl.pallas_call(
        body,
        grid=(m // tm,),
        in_specs=[
            pl.BlockSpec((tm, d), lambda i: (i, 0)),
            pl.BlockSpec((1, d), lambda i: (0, 0)),
            _mod_spec(3, tps, d),
            _mod_spec(4, tps, d),
            _mod_spec(5, tps, d),
            pl.BlockSpec((d, 2 * hidden), lambda i: (0, 0), pipeline_mode=resident),
            pl.BlockSpec((hidden, d), lambda i: (0, 0), pipeline_mode=resident),
        ],
        out_specs=pl.BlockSpec((tm, d), lambda i: (i, 0)),
        out_shape=jax.ShapeDtypeStruct((m, d), f32),
        scratch_shapes=[pltpu.VMEM((tm, d), bf16)],
        compiler_params=_params("arbitrary"),
        name="ffn",
    )(x, norm_g, mod_l, mod_l, mod_l, w_gate_up, w_down)


def _rope_tables(seq, dh):
    rope_dim = dh // 4
    half = rope_dim // 2
    inv = jnp.exp(-math.log(ROPE_THETA) * (2.0 * jnp.arange(half, dtype=f32) / rope_dim))
    ang = jnp.arange(seq, dtype=f32)[:, None] * inv[None, :]
    cos = jnp.cos(ang)
    sin = jnp.sin(ang)
    ones = jnp.ones((seq, dh // 2 - half), f32)
    zeros = jnp.zeros((seq, dh // 2 - half), f32)
    cos_t = jnp.concatenate([cos, ones, cos, ones], axis=-1)
    sin_t = jnp.concatenate([-sin, zeros, sin, zeros], axis=-1)
    return cos_t, sin_t


def _rope_lane_order(t, dh):
    half = (dh // 4) // 2
    return jnp.concatenate([t[..., :half], t[..., 2 * half:dh // 2 + half],
                            t[..., half:2 * half], t[..., dh // 2 + half:]], axis=-1)


def kernel(x, c, mod_w, mod_b, mix_norm_g, ffn_norm_g, gdn_w_in, gdn_conv_w, gdn_A_log,
           gdn_dt_bias, gdn_norm_g, gdn_w_out, dsw_w_in, dsw_q_norm_g, dsw_k_norm_g,
           dsw_w_out, ffn_w_gate_up, ffn_w_down):
    batch, seq, d = x.shape
    depth = mod_w.shape[0]
    m = batch * seq
    heads = GDN_HEADS
    dh = d // DSW_HEADS
    assert seq % ROW_TILE == 0 and seq % GDN_TIME_BLOCK == 0 and d == heads * LANES

    mod_rows = 16
    c_pad = jnp.zeros((mod_rows, d), f32).at[:batch].set(c)
    mod = _modulation(c_pad, mod_w, mod_b)[:, :batch]
    mod = mod.reshape(depth, batch, 6, 1, d)

    cos_t, sin_t = _rope_tables(seq, dh)
    xf = x.reshape(m, d)

    for layer in range(depth):
        mod_l = mod[layer]
        j = layer // N_MIXERS
        norm_g = mix_norm_g[layer].reshape(1, d)
        if layer % N_MIXERS == 0:
            n_conv = 3 * d
            w_in = gdn_w_in[j]
            w4 = w_in[:, :4 * d].reshape(d, 4, d).transpose(1, 0, 2).astype(bf16)
            wba = jnp.zeros((d, LANES), f32).at[:, :2 * heads].set(w_in[:, 4 * d:]).astype(bf16)
            cw = jnp.zeros((GDN_CONV, 4 * d), f32).at[:, :n_conv].set(gdn_conv_w[j])
            cw = cw.reshape(GDN_CONV, 4, d).transpose(1, 0, 2)
            gp = jnp.zeros((SUBLANES, LANES), f32)
            gp = gp.at[0, heads:2 * heads].set(gdn_A_log[j]).at[1, heads:2 * heads].set(gdn_dt_bias[j])
            qkvz, ba = _gdn_in_proj(xf, mod_l, norm_g, w4, wba, cw, gp, seq=seq)
            o = _gdn_core(qkvz, ba, gdn_norm_g[j].reshape(1, -1), batch=batch, seq=seq)
            xf = _out_proj(xf, o, gdn_w_out[j].astype(bf16), mod_l, 2, seq=seq)
        else:
            n_groups = len(DSW_PATTERNS)
            w5 = dsw_w_in[j].reshape(d, n_groups, 3, DSW_HEADS, dh)
            w5 = jnp.concatenate([_rope_lane_order(w5[:, :, :2], dh), w5[:, :, 2:]], axis=2)
            w_in = w5.reshape(d, 3 * n_groups, d).transpose(1, 0, 2).astype(bf16)
            gains = []
            for gi in range(n_groups):
                q_gain = _rope_lane_order(dsw_q_norm_g[j, gi], dh) * (dh ** -0.5)
                gains.append(jnp.tile(q_gain, DSW_HEADS))
                gains.append(jnp.tile(_rope_lane_order(dsw_k_norm_g[j, gi], dh), DSW_HEADS))
                gains.append(jnp.ones((d,), f32))
            gn = jnp.stack(gains).reshape(3 * n_groups, 1, d)
            qkvs = _dsw_in_proj(xf, mod_l, norm_g, w_in, gn, cos_t, sin_t, batch=batch, seq=seq)
            outs, lses = [], []
            for gi, (window, dilation) in enumerate(DSW_PATTERNS):
                o_g, lse_g = _dsw_attention(qkvs[gi], window, dilation, d=d)
                outs.append(o_g)
                lses.append(lse_g)
            xf = _dsw_out_proj(xf, outs, lses, dsw_w_out[j].astype(bf16), mod_l, seq=seq)
        xf = _ffn(xf, mod_l, ffn_norm_g[layer].reshape(1, d), ffn_w_gate_up[layer].astype(bf16),
                  ffn_w_down[layer].astype(bf16), seq=seq)
    return xf.reshape(batch, seq, d)
```

```python
import functools
import math

import jax
import jax.numpy as jnp
from jax import lax
from jax.experimental import pallas as pl
from jax.experimental.pallas import tpu as pltpu

EPS = 1e-6
N_MIXERS = 2
GDN_HEADS = 8
GDN_CONV = 4
GDN_CHUNK = 64
GDN_SOLVE_BLOCK = 16
GDN_SCAN_CHUNKS = 2
GDN_PREP_CHUNKS = 4
DSW_PATTERNS = ((128, 1), (512, 4), (2048, 16))
DSW_HEADS = 8
DSW_BLOCK = 128
DSW_BLOCKS_PER_STEP = 8
ROPE_THETA = 500000.0

LANES = 128
SUBLANES = 8
VMEM_LIMIT_BYTES = 56 * 1024 * 1024

ROW_TILE = 1024
DSW_OUT_ROW_TILE = 512
GDN_TIME_BLOCK = 1024
FFN_HIDDEN_TILE = 256

f32 = jnp.float32
bf16 = jnp.bfloat16


def _params(*sem):
    return pltpu.CompilerParams(dimension_semantics=sem, vmem_limit_bytes=VMEM_LIMIT_BYTES)


def _dot(a, b):
    return jnp.dot(a, b, preferred_element_type=f32)


def _dot_nt(a, b):
    return lax.dot_general(a, b, (((1,), (1,)), ((), ())), preferred_element_type=f32)


def _dot_tn(a, b):
    return lax.dot_general(a, b, (((0,), (0,)), ((), ())), preferred_element_type=f32)


def _dot_round(a, b):
    return _dot(a.astype(bf16), b.astype(bf16))


def _silu(x):
    return x * (0.5 + 0.5 * jnp.tanh(0.5 * x))


def _softplus(x):
    return jnp.maximum(x, 0.0) + jnp.log1p(jnp.exp(-jnp.abs(x)))


def _norm_modulate(x, gain, shift, scale):
    ms = jnp.mean(x * x, axis=-1, keepdims=True)
    y = x * lax.rsqrt(ms + EPS) * gain
    return y * (1.0 + scale) + shift


def _mod_body(c_ref, w_ref, b_ref, o_ref):
    c = c_ref[...]
    cond = _silu(c).astype(bf16)
    o_ref[...] = _dot(cond, w_ref[...].astype(bf16)) + b_ref[...]


def _modulation(c_pad, mod_w, mod_b):
    depth, d, n = mod_w.shape
    rows = c_pad.shape[0]
    tn = 1536
    return pl.pallas_call(
        _mod_body,
        grid=(depth, n // tn),
        in_specs=[
            pl.BlockSpec((rows, d), lambda l, j: (0, 0)),
            pl.BlockSpec((None, d, tn), lambda l, j: (l, 0, j)),
            pl.BlockSpec((None, 1, tn), lambda l, j: (l, 0, j)),
        ],
        out_specs=pl.BlockSpec((None, rows, tn), lambda l, j: (l, 0, j)),
        out_shape=jax.ShapeDtypeStruct((depth, rows, n), f32),
        compiler_params=_params("arbitrary", "arbitrary"),
        name="modulation",
    )(c_pad, mod_w, mod_b.reshape(depth, 1, n))


def _mod_spec(idx, tiles_per_seq, d):
    return pl.BlockSpec((None, None, 1, d), lambda i, *_: (i // tiles_per_seq, idx, 0, 0))


def _gdn_in_body(x_ref, ng_ref, sh_ref, sc_ref, w_ref, wba_ref, cw_ref, gp_ref,
                 out_ref, ba_ref, h_scr, ya_scr, yb_scr, tail_scr, *, tiles_per_seq, tm, dk):
    i = pl.program_id(0)
    n_tiles = w_ref.shape[0]
    y_scr = (ya_scr, yb_scr)
    first = (i % tiles_per_seq) == 0
    n_conv_tiles = n_tiles - 1

    @pl.when(i == 0)
    def _():
        tail_scr[...] = jnp.zeros(tail_scr.shape, f32)

    def epilogue(tile):
        y_ref = y_scr[tile % 2]
        if tile == n_conv_tiles:
            out_ref[tile] = y_ref[SUBLANES:, :].astype(bf16)
            return
        y_ref[0:SUBLANES, :] = jnp.where(first, 0.0, tail_scr[tile])
        tail_scr[tile] = y_ref[tm:, :]
        post = dk ** -0.5 if tile == 0 else 1.0
        for h in range(GDN_HEADS):
            sl = slice(h * dk, (h + 1) * dk)
            acc = cw_ref[tile, GDN_CONV - 1:GDN_CONV, sl] * y_ref[SUBLANES:, sl]
            for kk in range(GDN_CONV - 1):
                start = SUBLANES - (GDN_CONV - 1) + kk
                acc = acc + cw_ref[tile, kk:kk + 1, sl] * y_ref[pl.ds(start, tm), sl]
            a = _silu(acc)
            if tile < 2:
                ss = jnp.sum(a * a, axis=-1, keepdims=True)
                a = a * (lax.rsqrt(ss + EPS) * post)
            out_ref[tile, :, sl] = a.astype(bf16)

    def matmul(tile):
        y_scr[tile % 2][SUBLANES:, :] = _dot(h_scr[...], w_ref[tile])

    h = _norm_modulate(x_ref[...], ng_ref[...], sh_ref[...], sc_ref[...])
    hb = h.astype(bf16)
    h_scr[...] = hb
    matmul(0)
    ba = _dot(hb, wba_ref[...])
    lane = lax.broadcasted_iota(jnp.int32, ba.shape, 1)
    beta = jax.nn.sigmoid(ba)
    g = -jnp.exp(gp_ref[0:1, :]) * _softplus(ba + gp_ref[1:2, :])
    ba_ref[...] = jnp.where(lane < GDN_HEADS, beta, g)
    for tile in range(n_tiles):
        if tile + 1 < n_tiles:
            matmul(tile + 1)
        epilogue(tile)


def _gdn_in_proj(x, mod_l, norm_g, w4, wba, cw, gp, *, seq):
    m, d = x.shape
    tm = ROW_TILE
    tps = seq // tm
    n_tiles, _, tn = w4.shape
    dk = tn // GDN_HEADS
    body = functools.partial(_gdn_in_body, tiles_per_seq=tps, tm=tm, dk=dk)
    resident = pl.Buffered(1)
    return pl.pallas_call(
        body,
        grid=(m // tm,),
        in_specs=[
            pl.BlockSpec((tm, d), lambda i: (i, 0)),
            pl.BlockSpec((1, d), lambda i: (0, 0)),
            _mod_spec(0, tps, d),
            _mod_spec(1, tps, d),
            pl.BlockSpec((n_tiles, d, tn), lambda i: (0, 0, 0), pipeline_mode=resident),
            pl.BlockSpec((d, LANES), lambda i: (0, 0)),
            pl.BlockSpec((n_tiles, GDN_CONV, tn), lambda i: (0, 0, 0)),
            pl.BlockSpec((SUBLANES, LANES), lambda i: (0, 0)),
        ],
        out_specs=[
            pl.BlockSpec((n_tiles, tm, tn), lambda i: (0, i, 0)),
            pl.BlockSpec((tm, LANES), lambda i: (i, 0)),
        ],
        out_shape=[
            jax.ShapeDtypeStruct((n_tiles, m, tn), bf16),
            jax.ShapeDtypeStruct((m, LANES), f32),
        ],
        scratch_shapes=[
            pltpu.VMEM((tm, d), bf16),
            pltpu.VMEM((tm + SUBLANES, tn), f32),
            pltpu.VMEM((tm + SUBLANES, tn), f32),
            pltpu.VMEM((n_tiles - 1, SUBLANES, tn), f32),
        ],
        compiler_params=_params("arbitrary"),
        name="gdn_in_proj",
    )(x, norm_g, mod_l, mod_l, w4, wba, cw, gp)


def _gdn_core_body(q_ref, k_ref, v_ref, z_ref, ba_ref, ng_ref, o_ref,
                   s_scr, u_scr, w_scr, qd_scr, kt_scr, a_scr, gl_scr, *, tb, dk):
    c_len = GDN_CHUNK
    n_chunks = tb // c_len
    heads = GDN_HEADS
    hs = range(heads)

    @pl.when(pl.program_id(1) == 0)
    def _():
        s_scr[...] = jnp.zeros(s_scr.shape, f32)

    row = lax.broadcasted_iota(jnp.int32, (c_len, c_len), 0)
    col = lax.broadcasted_iota(jnp.int32, (c_len, c_len), 1)
    causal = row >= col
    strict = row > col
    tril_f = causal.astype(f32)
    eye_f = (row == col).astype(f32)
    diag_blk = (row // GDN_SOLVE_BLOCK) == (col // GDN_SOLVE_BLOCK)

    def prep(cp, carry):
        x, rhs, decay, where = [], [], [], []
        for ci in range(GDN_PREP_CHUNKS):
            c = cp * GDN_PREP_CHUNKS + ci
            rows = pl.ds(pl.multiple_of(c * c_len, c_len), c_len)
            g8 = pl.multiple_of(c * SUBLANES, SUBLANES)
            ba = ba_ref[rows, :]
            gc_all = jnp.dot(tril_f, ba, precision=lax.Precision.HIGHEST,
                             preferred_element_type=f32)
            gc_t = gc_all.T
            for h in hs:
                sl = slice(h * dk, (h + 1) * dk)
                q = q_ref[rows, sl]
                k = k_ref[rows, sl]
                beta = ba[:, h:h + 1]
                gc = gc_all[:, heads + h:heads + h + 1]
                gr = gc_t[heads + h:heads + h + 1, :]
                g_last = gc_all[c_len - 1:c_len, heads + h:heads + h + 1]
                dec = jnp.where(causal, jnp.exp(jnp.minimum(gc - gr, 0.0)), 0.0)
                eg = jnp.exp(gc)
                kf = k.astype(f32)
                kb = kf * beta
                kk = _dot_nt(kb.astype(bf16), k)
                x.append(jnp.where(strict, -(kk * dec), 0.0))
                rhs.append(jnp.concatenate([v_ref[rows, sl].astype(f32) * beta, kb * eg], axis=-1))
                decay.append(dec)
                where.append((h, rows, sl))
                qd_scr[h, rows, :] = (q.astype(f32) * eg).astype(bf16)
                kt_scr[h, rows, :] = (kf * jnp.exp(g_last - gc)).astype(bf16)
                gl_scr[h, pl.ds(g8, SUBLANES), :] = jnp.broadcast_to(jnp.exp(g_last), (SUBLANES, dk))
        items = range(len(x))
        for it in items:
            h, rows, sl = where[it]
            a_in = jnp.where(causal, _dot_nt(q_ref[rows, sl], k_ref[rows, sl]) * decay[it], 0.0)
            a_scr[h, rows, :] = a_in.astype(bf16)
        x_d = [jnp.where(diag_blk, x[it], 0.0) for it in items]
        d_inv = [eye_f + x_d[it] for it in items]
        xp = x_d
        for _ in range(int(math.log2(GDN_SOLVE_BLOCK)) - 1):
            xp = [_dot_round(xp[it], xp[it]) for it in items]
            d_inv = [d_inv[it] + _dot_round(d_inv[it], xp[it]) for it in items]
        mn = [_dot_round(d_inv[it], x[it] - x_d[it]) for it in items]
        o_inv = [eye_f + mn[it] for it in items]
        mp = mn
        for _ in range(int(math.log2(c_len // GDN_SOLVE_BLOCK)) - 1):
            mp = [_dot_round(mp[it], mp[it]) for it in items]
            o_inv = [o_inv[it] + _dot_round(o_inv[it], mp[it]) for it in items]
        t_inv = [_dot_round(o_inv[it], d_inv[it]) for it in items]
        sol = [_dot_round(t_inv[it], rhs[it]) for it in items]
        for it in items:
            h, rows, _ = where[it]
            u_scr[h, rows, :] = sol[it][:, :dk]
            w_scr[h, rows, :] = sol[it][:, dk:].astype(bf16)
        return carry

    lax.fori_loop(0, n_chunks // GDN_PREP_CHUNKS, prep, 0)

    ng = ng_ref[...]

    def scan_chunk(c, s):
        rows = pl.ds(pl.multiple_of(c * c_len, c_len), c_len)
        g8 = pl.multiple_of(c * SUBLANES, SUBLANES)
        sb = [s[h].astype(bf16) for h in hs]
        ws = [_dot(w_scr[h, rows, :], sb[h]) for h in hs]
        qs = [_dot(qd_scr[h, rows, :], sb[h]) for h in hs]
        v_new = [(u_scr[h, rows, :] - ws[h]).astype(bf16) for h in hs]
        s_new = [s[h] * gl_scr[h, pl.ds(g8, 1), :] + _dot_tn(kt_scr[h, rows, :], v_new[h]) for h in hs]
        o = [qs[h] + _dot(a_scr[h, rows, :], v_new[h]) for h in hs]
        for h in hs:
            sl = slice(h * dk, (h + 1) * dk)
            z = z_ref[rows, sl].astype(f32)
            ms = jnp.mean(o[h] * o[h], axis=-1, keepdims=True)
            o_ref[rows, sl] = (o[h] * lax.rsqrt(ms + EPS) * ng * _silu(z)).astype(bf16)
        return s_new

    def scan(cp, carry):
        s = [s_scr[h] for h in hs]
        for ci in range(GDN_SCAN_CHUNKS):
            s = scan_chunk(cp * GDN_SCAN_CHUNKS + ci, s)
        for h in hs:
            s_scr[h] = s[h]
        return carry

    lax.fori_loop(0, n_chunks // GDN_SCAN_CHUNKS, scan, 0)


def _gdn_core(qkvz, ba, norm_g, *, batch, seq):
    _, m, d = qkvz.shape
    dk = d // GDN_HEADS
    tb = GDN_TIME_BLOCK
    nt = seq // tb
    n_chunks = tb // GDN_CHUNK

    def col(jc):
        return pl.BlockSpec((None, tb, d), lambda b, t: (jc, b * nt + t, 0))

    body = functools.partial(_gdn_core_body, tb=tb, dk=dk)
    return pl.pallas_call(
        body,
        grid=(batch, nt),
        in_specs=[col(0), col(1), col(2), col(3),
                  pl.BlockSpec((tb, LANES), lambda b, t: (b * nt + t, 0)),
                  pl.BlockSpec((1, dk), lambda b, t: (0, 0))],
        out_specs=pl.BlockSpec((tb, d), lambda b, t: (b * nt + t, 0)),
        out_shape=jax.ShapeDtypeStruct((m, d), bf16),
        scratch_shapes=[
            pltpu.VMEM((GDN_HEADS, dk, dk), f32),
            pltpu.VMEM((GDN_HEADS, tb, dk), f32),
            pltpu.VMEM((GDN_HEADS, tb, dk), bf16),
            pltpu.VMEM((GDN_HEADS, tb, dk), bf16),
            pltpu.VMEM((GDN_HEADS, tb, dk), bf16),
            pltpu.VMEM((GDN_HEADS, tb, GDN_CHUNK), bf16),
            pltpu.VMEM((GDN_HEADS, n_chunks * SUBLANES, dk), f32),
        ],
        compiler_params=_params("arbitrary", "arbitrary"),
        name="gdn_core",
    )(qkvz, qkvz, qkvz, qkvz, ba, norm_g)


def _out_proj_body(x_ref, o_ref, w_ref, g_ref, out_ref):
    out_ref[...] = x_ref[...] + g_ref[...] * _dot(o_ref[...], w_ref[...])


def _out_proj(x, o, w, mod_l, gate_idx, *, seq):
    m, d = x.shape
    tm = ROW_TILE
    tps = seq // tm
    return pl.pallas_call(
        _out_proj_body,
        grid=(m // tm,),
        in_specs=[
            pl.BlockSpec((tm, d), lambda i: (i, 0)),
            pl.BlockSpec((tm, d), lambda i: (i, 0)),
            pl.BlockSpec((d, d), lambda i: (0, 0)),
            _mod_spec(gate_idx, tps, d),
        ],
        out_specs=pl.BlockSpec((tm, d), lambda i: (i, 0)),
        out_shape=jax.ShapeDtypeStruct((m, d), f32),
        compiler_params=_params("arbitrary"),
        name="out_proj",
    )(x, o, w, mod_l)


def _dsw_in_body(x_ref, ng_ref, sh_ref, sc_ref, w_ref, gn_ref, cos_ref, sin_ref,
                 o0_ref, o1_ref, o2_ref, h_scr, ya_scr, yb_scr, slab_scr, *, dh, tm, n_tiles):
    j = pl.program_id(1)
    y_scr = (ya_scr, yb_scr)
    out_refs = (o0_ref, o1_ref, o2_ref)

    def write_head(g, h, val):
        dil = DSW_PATTERNS[g][1]
        n = tm // dil
        sl = slice(h * dh, (h + 1) * dh)
        if dil == 1:
            out_refs[g][0, :, sl] = val.astype(bf16)
        else:
            slab_scr[h] = val
            for r in range(dil):
                out_refs[g][r, :, sl] = slab_scr[h, pl.ds(r, n, stride=dil), :].astype(bf16)

    def epilogue(tile):
        g, which = divmod(tile, 3)
        y_ref = y_scr[tile % 2]
        if which == 2:
            for h in range(DSW_HEADS):
                write_head(g, h, y_ref[:, h * dh:(h + 1) * dh])
            return
        cos_t = cos_ref[...]
        sin_t = sin_ref[...]
        for h in range(DSW_HEADS):
            sl = slice(h * dh, (h + 1) * dh)
            yh = y_ref[:, sl]
            ms = jnp.mean(yh * yh, axis=-1, keepdims=True)
            yn = yh * lax.rsqrt(ms + EPS) * gn_ref[:, sl]
            write_head(g, h, yn * cos_t + pltpu.roll(yn, dh // 2, 1) * sin_t)

    for step in range(n_tiles + 1):
        @pl.when(j == step)
        def _(step=step):
            if step == 0:
                h = _norm_modulate(x_ref[...], ng_ref[...], sh_ref[...], sc_ref[...])
                h_scr[...] = h.astype(bf16)
            if step < n_tiles:
                y_scr[step % 2][...] = _dot(h_scr[...], w_ref[...])
            if step > 0:
                epilogue(step - 1)


def _dsw_in_proj(x, mod_l, norm_g, w, gn, cos_t, sin_t, *, batch, seq):
    m, d = x.shape
    tm = ROW_TILE
    tps = seq // tm
    n_tiles, _, tn = w.shape
    dh = d // DSW_HEADS
    body = functools.partial(_dsw_in_body, dh=dh, tm=tm, n_tiles=n_tiles)
    rope_spec = pl.BlockSpec((tm, dh), lambda i, j: (i % tps, 0))

    def out_spec(g):
        dil = DSW_PATTERNS[g][1]
        return pl.BlockSpec((None, dil, None, tm // dil, tn),
                            lambda i, j: (i // tps, 0, jnp.clip(j - 1 - 3 * g, 0, 2), i % tps, 0))

    groups = range(len(DSW_PATTERNS))
    return pl.pallas_call(
        body,
        grid=(m // tm, n_tiles + 1),
        in_specs=[
            pl.BlockSpec((tm, d), lambda i, j: (i, 0)),
            pl.BlockSpec((1, d), lambda i, j: (0, 0)),
            _mod_spec(0, tps, d),
            _mod_spec(1, tps, d),
            pl.BlockSpec((None, d, tn), lambda i, j: (jnp.minimum(j, n_tiles - 1), 0, 0)),
            pl.BlockSpec((None, 1, tn), lambda i, j: (jnp.maximum(j - 1, 0), 0, 0)),
            rope_spec, rope_spec,
        ],
        out_specs=[out_spec(g) for g in groups],
        out_shape=[jax.ShapeDtypeStruct((batch, DSW_PATTERNS[g][1], 3, seq // DSW_PATTERNS[g][1], d), bf16)
                   for g in groups],
        scratch_shapes=[pltpu.VMEM((tm, d), bf16), pltpu.VMEM((tm, tn), f32), pltpu.VMEM((tm, tn), f32),
                        pltpu.VMEM((DSW_HEADS, tm, dh), f32)],
        compiler_params=_params("arbitrary", "arbitrary"),
        name="dsw_in_proj",
    )(x, norm_g, mod_l, mod_l, w, gn, cos_t, sin_t)


def _dsw_attn_body(q_ref, kp_ref, kc_ref, vp_ref, vc_ref, o_ref, lse_ref, *, dh, span, nq):
    first_step = pl.program_id(2) == 0
    p_len = DSW_BLOCK
    qi = lax.broadcasted_iota(jnp.int32, (p_len, 2 * p_len), 0)
    ki = lax.broadcasted_iota(jnp.int32, (p_len, 2 * p_len), 1)
    rel = qi + p_len - ki
    band = (rel >= 0) & (rel <= span)
    lane = lax.broadcasted_iota(jnp.int32, (p_len, LANES), 1)
    hs = range(DSW_HEADS)

    def prev_and_cur(prev_ref, cur_ref, sb, sl):
        cur = cur_ref[sb * p_len:(sb + 1) * p_len, sl]
        prev = prev_ref[:, sl] if sb == 0 else cur_ref[(sb - 1) * p_len:sb * p_len, sl]
        return jnp.concatenate([prev, cur], axis=0)

    def block_scores(sb):
        return [_dot_nt(q_ref[sb * p_len:(sb + 1) * p_len, h * dh:(h + 1) * dh],
                        prev_and_cur(kp_ref, kc_ref, sb, slice(h * dh, (h + 1) * dh))) for h in hs]

    scores = block_scores(0)
    for sb in range(nq):
        nxt = block_scores(sb + 1) if sb + 1 < nq else None
        rows = slice(sb * p_len, (sb + 1) * p_len)
        valid = band & (jnp.logical_not(first_step) | (ki >= p_len)) if sb == 0 else band
        lse_tile = jnp.zeros((p_len, LANES), f32)
        for h in hs:
            sl = slice(h * dh, (h + 1) * dh)
            s = jnp.where(valid, scores[h], -jnp.inf)
            mx = jnp.max(s, axis=-1, keepdims=True)
            p = jnp.exp(s - mx)
            l = jnp.sum(p, axis=-1, keepdims=True)
            o = _dot(p.astype(bf16), prev_and_cur(vp_ref, vc_ref, sb, sl)) * (1.0 / l)
            o_ref[rows, sl] = o.astype(bf16)
            lse_tile = jnp.where(lane == h, mx + jnp.log(l), lse_tile)
        lse_ref[rows, :] = lse_tile
        scores = nxt


def _dsw_attention(qkv, window, dilation, *, d):
    batch, _, _, ts, _ = qkv.shape
    dh = d // DSW_HEADS
    nb = ts // DSW_BLOCK
    nq = min(DSW_BLOCKS_PER_STEP, nb)
    span = window // dilation

    def in_spec(which, prev):
        if prev:
            return pl.BlockSpec((None, None, None, DSW_BLOCK, d),
                                lambda b, r, s: (b, r, which, jnp.maximum(s * nq - 1, 0), 0))
        return pl.BlockSpec((None, None, None, nq * DSW_BLOCK, d), lambda b, r, s: (b, r, which, s, 0))

    body = functools.partial(_dsw_attn_body, dh=dh, span=span, nq=nq)
    return pl.pallas_call(
        body,
        grid=(batch, dilation, nb // nq),
        in_specs=[in_spec(0, False), in_spec(1, True), in_spec(1, False),
                  in_spec(2, True), in_spec(2, False)],
        out_specs=[
            pl.BlockSpec((None, None, nq * DSW_BLOCK, d), lambda b, r, s: (b, r, s, 0)),
            pl.BlockSpec((None, None, nq * DSW_BLOCK, LANES), lambda b, r, s: (b, r, s, 0)),
        ],
        out_shape=[
            jax.ShapeDtypeStruct((batch, dilation, ts, d), bf16),
            jax.ShapeDtypeStruct((batch, dilation, ts, LANES), f32),
        ],
        compiler_params=_params("arbitrary", "arbitrary", "arbitrary"),
        name=f"dsw_attn_d{dilation}",
    )(qkv, qkv, qkv, qkv, qkv)


def _dsw_out_body(x_ref, o0_ref, o1_ref, o2_ref, l0_ref, l1_ref, l2_ref, w_ref, g_ref,
                  out_ref, lse_scr, nat_scr, om_scr, *, dh, tm):
    o_refs = (o0_ref, o1_ref, o2_ref)
    l_refs = (l0_ref, l1_ref, l2_ref)
    groups = range(len(DSW_PATTERNS))

    def token_rows(g, r):
        dil = DSW_PATTERNS[g][1]
        return pl.ds(r, tm // dil, stride=dil)

    for g in groups:
        for r in range(DSW_PATTERNS[g][1]):
            lse_scr[g, token_rows(g, r), :] = l_refs[g][r]
    lse = [lse_scr[g] for g in groups]
    mx = jnp.maximum(jnp.maximum(lse[0], lse[1]), lse[2])
    e = [jnp.exp(lse[g] - mx) for g in groups]
    inv = 1.0 / (e[0] + e[1] + e[2])
    wts = [e[g] * inv for g in groups]
    for h in range(DSW_HEADS):
        sl = slice(h * dh, (h + 1) * dh)
        om = None
        for g in groups:
            for r in range(DSW_PATTERNS[g][1]):
                nat_scr[g, token_rows(g, r), :] = o_refs[g][r, :, sl].astype(f32)
            term = wts[g][:, h:h + 1] * nat_scr[g]
            om = term if om is None else om + term
        om_scr[:, sl] = om.astype(bf16)
    out_ref[...] = x_ref[...] + g_ref[...] * _dot(om_scr[...], w_ref[...])


def _dsw_out_proj(x, outs, lses, w, mod_l, *, seq):
    m, d = x.shape
    tm = DSW_OUT_ROW_TILE
    tps = seq // tm
    dh = d // DSW_HEADS
    row = pl.BlockSpec((tm, d), lambda i: (i, 0))

    def stream_spec(g, width):
        dil = DSW_PATTERNS[g][1]
        return pl.BlockSpec((None, dil, tm // dil, width), lambda i: (i // tps, 0, i % tps, 0))

    groups = range(len(DSW_PATTERNS))
    body = functools.partial(_dsw_out_body, dh=dh, tm=tm)
    return pl.pallas_call(
        body,
        grid=(m // tm,),
        in_specs=[row] + [stream_spec(g, d) for g in groups] + [stream_spec(g, LANES) for g in groups]
                 + [pl.BlockSpec((d, d), lambda i: (0, 0)), _mod_spec(2, tps, d)],
        out_specs=row,
        out_shape=jax.ShapeDtypeStruct((m, d), f32),
        scratch_shapes=[pltpu.VMEM((len(DSW_PATTERNS), tm, LANES), f32),
                        pltpu.VMEM((len(DSW_PATTERNS), tm, dh), f32),
                        pltpu.VMEM((tm, d), bf16)],
        compiler_params=_params("arbitrary"),
        name="dsw_out_proj",
    )(x, *outs, *lses, w, mod_l)


def _ffn_body(x_ref, ng_ref, sh_ref, sc_ref, g_ref, wgu_ref, wd_ref, out_ref, h_scr, *, th):
    hidden = wd_ref.shape[0]
    h = _norm_modulate(x_ref[...], ng_ref[...], sh_ref[...], sc_ref[...])
    h_scr[...] = h.astype(bf16)
    hb = h_scr[...]
    acc = None
    act_prev = None
    for t in range(hidden // th + 1):
        if t * th < hidden:
            gate = _dot(hb, wgu_ref[:, t * th:(t + 1) * th])
            up = _dot(hb, wgu_ref[:, hidden + t * th:hidden + (t + 1) * th])
        if act_prev is not None:
            part = _dot(act_prev, wd_ref[(t - 1) * th:t * th, :])
            acc = part if acc is None else acc + part
        if t * th < hidden:
            act_prev = (_silu(gate) * up).astype(bf16)
    out_ref[...] = x_ref[...] + g_ref[...] * acc


def _ffn(x, mod_l, norm_g, w_gate_up, w_down, *, seq):
    m, d = x.shape
    tm = ROW_TILE
    tps = seq // tm
    hidden = w_down.shape[0]
    resident = pl.Buffered(1)
    body = functools.partial(_ffn_body, th=FFN_HIDDEN_TILE)
    return pl.pallas_call(
        body,
        grid=(m // tm,),
        in_specs=[
            pl.BlockSpec((tm, d), lambda i: (i, 0)),
            pl.BlockSpec((1, d), lambda i: (0, 0)),
            _mod_spec(3, tps, d),
            _mod_spec(4, tps, d),
            _mod_spec(5, tps, d),
            pl.BlockSpec((d, 2 * hidden), lambda i: (0, 0), pipeline_mode=resident),
            pl.BlockSpec((hidden, d), lambda i: (0, 0), pipeline_mode=resident),
        ],
        out_specs=pl.BlockSpec((tm, d), lambda i: (i, 0)),
        out_shape=jax.ShapeDtypeStruct((m, d), f32),
        scratch_shapes=[pltpu.VMEM((tm, d), bf16)],
        compiler_params=_params("arbitrary"),
        name="ffn",
    )(x, norm_g, mod_l, mod_l, mod_l, w_gate_up, w_down)


def _rope_tables(seq, dh):
    rope_dim = dh // 4
    half = rope_dim // 2
    inv = jnp.exp(-math.log(ROPE_THETA) * (2.0 * jnp.arange(half, dtype=f32) / rope_dim))
    ang = jnp.arange(seq, dtype=f32)[:, None] * inv[None, :]
    cos = jnp.cos(ang)
    sin = jnp.sin(ang)
    ones = jnp.ones((seq, dh // 2 - half), f32)
    zeros = jnp.zeros((seq, dh // 2 - half), f32)
    cos_t = jnp.concatenate([cos, ones, cos, ones], axis=-1)
    sin_t = jnp.concatenate([-sin, zeros, sin, zeros], axis=-1)
    return cos_t, sin_t


def _rope_lane_order(t, dh):
    half = (dh // 4) // 2
    return jnp.concatenate([t[..., :half], t[..., 2 * half:dh // 2 + half],
                            t[..., half:2 * half], t[..., dh // 2 + half:]], axis=-1)


def kernel(x, c, mod_w, mod_b, mix_norm_g, ffn_norm_g, gdn_w_in, gdn_conv_w, gdn_A_log,
           gdn_dt_bias, gdn_norm_g, gdn_w_out, dsw_w_in, dsw_q_norm_g, dsw_k_norm_g,
           dsw_w_out, ffn_w_gate_up, ffn_w_down):
    batch, seq, d = x.shape
    depth = mod_w.shape[0]
    m = batch * seq
    heads = GDN_HEADS
    dh = d // DSW_HEADS
    assert seq % ROW_TILE == 0 and seq % GDN_TIME_BLOCK == 0 and d == heads * LANES

    mod_rows = 16
    c_pad = jnp.zeros((mod_rows, d), f32).at[:batch].set(c)
    mod = _modulation(c_pad, mod_w, mod_b)[:, :batch]
    mod = mod.reshape(depth, batch, 6, 1, d)

    cos_t, sin_t = _rope_tables(seq, dh)
    xf = x.reshape(m, d)

    for layer in range(depth):
        mod_l = mod[layer]
        j = layer // N_MIXERS
        norm_g = mix_norm_g[layer].reshape(1, d)
        if layer % N_MIXERS == 0:
            n_conv = 3 * d
            w_in = gdn_w_in[j]
            w4 = w_in[:, :4 * d].reshape(d, 4, d).transpose(1, 0, 2).astype(bf16)
            wba = jnp.zeros((d, LANES), f32).at[:, :2 * heads].set(w_in[:, 4 * d:]).astype(bf16)
            cw = jnp.zeros((GDN_CONV, 4 * d), f32).at[:, :n_conv].set(gdn_conv_w[j])
            cw = cw.reshape(GDN_CONV, 4, d).transpose(1, 0, 2)
            gp = jnp.zeros((SUBLANES, LANES), f32)
            gp = gp.at[0, heads:2 * heads].set(gdn_A_log[j]).at[1, heads:2 * heads].set(gdn_dt_bias[j])
            qkvz, ba = _gdn_in_proj(xf, mod_l, norm_g, w4, wba, cw, gp, seq=seq)
            o = _gdn_core(qkvz, ba, gdn_norm_g[j].reshape(1, -1), batch=batch, seq=seq)
            xf = _out_proj(xf, o, gdn_w_out[j].astype(bf16), mod_l, 2, seq=seq)
        else:
            n_groups = len(DSW_PATTERNS)
            w5 = dsw_w_in[j].reshape(d, n_groups, 3, DSW_HEADS, dh)
            w5 = jnp.concatenate([_rope_lane_order(w5[:, :, :2], dh), w5[:, :, 2:]], axis=2)
            w_in = w5.reshape(d, 3 * n_groups, d).transpose(1, 0, 2).astype(bf16)
            gains = []
            for gi in range(n_groups):
                q_gain = _rope_lane_order(dsw_q_norm_g[j, gi], dh) * (dh ** -0.5)
                gains.append(jnp.tile(q_gain, DSW_HEADS))
                gains.append(jnp.tile(_rope_lane_order(dsw_k_norm_g[j, gi], dh), DSW_HEADS))
                gains.append(jnp.ones((d,), f32))
            gn = jnp.stack(gains).reshape(3 * n_groups, 1, d)
            qkvs = _dsw_in_proj(xf, mod_l, norm_g, w_in, gn, cos_t, sin_t, batch=batch, seq=seq)
            outs, lses = [], []
            for gi, (window, dilation) in enumerate(DSW_PATTERNS):
                o_g, lse_g = _dsw_attention(qkvs[gi], window, dilation, d=d)
                outs.append(o_g)
                lses.append(lse_g)
            xf = _dsw_out_proj(xf, outs, lses, dsw_w_out[j].astype(bf16), mod_l, seq=seq)
        xf = _ffn(xf, mod_l, ffn_norm_g[layer].reshape(1, d), ffn_w_gate_up[layer].astype(bf16),
                  ffn_w_down[layer].astype(bf16), seq=seq)
    return xf.reshape(batch, seq, d)
```

```python
import functools
import math

import jax
import jax.numpy as jnp
from jax import lax
from jax.experimental import pallas as pl
from jax.experimental.pallas import tpu as pltpu

EPS = 1e-6
N_MIXERS = 2
GDN_HEADS = 8
GDN_CONV = 4
GDN_CHUNK = 64
GDN_SOLVE_BLOCK = 16
GDN_BATCH_GROUP = 2
GDN_SCAN_CHUNKS = 2
GDN_PREP_CHUNKS = 4
DSW_PATTERNS = ((128, 1), (512, 4), (2048, 16))
DSW_HEADS = 8
DSW_BLOCK = 128
DSW_BLOCKS_PER_STEP = 4
ROPE_THETA = 500000.0

LANES = 128
SUBLANES = 8
VMEM_LIMIT_BYTES = 56 * 1024 * 1024

ROW_TILE = 1024
DSW_MAX_LOAD_STRIDE = 4
DSW_OUT_ROW_TILE = 512
GDN_TIME_BLOCK = 512
FFN_HIDDEN_TILE = 256

f32 = jnp.float32
bf16 = jnp.bfloat16


def _params(*sem):
    return pltpu.CompilerParams(dimension_semantics=sem, vmem_limit_bytes=VMEM_LIMIT_BYTES)


def _dot(a, b):
    return jnp.dot(a, b, preferred_element_type=f32)


def _dot_nt(a, b):
    return lax.dot_general(a, b, (((1,), (1,)), ((), ())), preferred_element_type=f32)


def _dot_tn(a, b):
    return lax.dot_general(a, b, (((0,), (0,)), ((), ())), preferred_element_type=f32)


def _dot_round(a, b):
    return _dot(a.astype(bf16), b.astype(bf16))


def _silu(x):
    return x * (0.5 + 0.5 * jnp.tanh(0.5 * x))


def _softplus(x):
    return jnp.maximum(x, 0.0) + jnp.log1p(jnp.exp(-jnp.abs(x)))


def _norm_modulate(x, gain, shift, scale):
    ms = jnp.mean(x * x, axis=-1, keepdims=True)
    y = x * lax.rsqrt(ms + EPS) * gain
    return y * (1.0 + scale) + shift


def _mod_body(c_ref, w_ref, b_ref, o_ref):
    c = c_ref[...]
    cond = _silu(c).astype(bf16)
    o_ref[...] = _dot(cond, w_ref[...].astype(bf16)) + b_ref[...]


def _modulation(c_pad, mod_w, mod_b):
    depth, d, n = mod_w.shape
    rows = c_pad.shape[0]
    tn = 1536
    return pl.pallas_call(
        _mod_body,
        grid=(depth, n // tn),
        in_specs=[
            pl.BlockSpec((rows, d), lambda l, j: (0, 0)),
            pl.BlockSpec((None, d, tn), lambda l, j: (l, 0, j)),
            pl.BlockSpec((None, 1, tn), lambda l, j: (l, 0, j)),
        ],
        out_specs=pl.BlockSpec((None, rows, tn), lambda l, j: (l, 0, j)),
        out_shape=jax.ShapeDtypeStruct((depth, rows, n), f32),
        compiler_params=_params("arbitrary", "arbitrary"),
        name="modulation",
    )(c_pad, mod_w, mod_b.reshape(depth, 1, n))


def _mod_spec(idx, tiles_per_seq, d):
    return pl.BlockSpec((None, None, 1, d), lambda i, *_: (i // tiles_per_seq, idx, 0, 0))


def _gdn_in_body(x_ref, ng_ref, sh_ref, sc_ref, w_ref, wba_ref, cw_ref, gp_ref,
                 out_ref, ba_ref, h_scr, ya_scr, yb_scr, tail_scr, *, tiles_per_seq, tm, dk):
    i = pl.program_id(0)
    n_tiles = w_ref.shape[0]
    y_scr = (ya_scr, yb_scr)
    first = (i % tiles_per_seq) == 0
    n_conv_tiles = n_tiles - 1

    @pl.when(i == 0)
    def _():
        tail_scr[...] = jnp.zeros(tail_scr.shape, f32)

    def epilogue(tile):
        y_ref = y_scr[tile % 2]
        if tile == n_conv_tiles:
            out_ref[tile] = y_ref[SUBLANES:, :].astype(bf16)
            return
        y_ref[0:SUBLANES, :] = jnp.where(first, 0.0, tail_scr[tile])
        tail_scr[tile] = y_ref[tm:, :]
        post = dk ** -0.5 if tile == 0 else 1.0
        for h in range(GDN_HEADS):
            sl = slice(h * dk, (h + 1) * dk)
            acc = cw_ref[tile, GDN_CONV - 1:GDN_CONV, sl] * y_ref[SUBLANES:, sl]
            for kk in range(GDN_CONV - 1):
                start = SUBLANES - (GDN_CONV - 1) + kk
                acc = acc + cw_ref[tile, kk:kk + 1, sl] * y_ref[pl.ds(start, tm), sl]
            a = _silu(acc)
            if tile < 2:
                ss = jnp.sum(a * a, axis=-1, keepdims=True)
                a = a * (lax.rsqrt(ss + EPS) * post)
            out_ref[tile, :, sl] = a.astype(bf16)

    def matmul(tile):
        y_scr[tile % 2][SUBLANES:, :] = _dot(h_scr[...], w_ref[tile])

    h = _norm_modulate(x_ref[...], ng_ref[...], sh_ref[...], sc_ref[...])
    hb = h.astype(bf16)
    h_scr[...] = hb
    matmul(0)
    ba = _dot(hb, wba_ref[...])
    lane = lax.broadcasted_iota(jnp.int32, ba.shape, 1)
    beta = jax.nn.sigmoid(ba)
    g = -jnp.exp(gp_ref[0:1, :]) * _softplus(ba + gp_ref[1:2, :])
    ba_ref[...] = jnp.where(lane < GDN_HEADS, beta, g)
    for tile in range(n_tiles):
        if tile + 1 < n_tiles:
            matmul(tile + 1)
        epilogue(tile)


def _gdn_in_proj(x, mod_l, norm_g, w4, wba, cw, gp, *, seq):
    m, d = x.shape
    tm = ROW_TILE
    tps = seq // tm
    n_tiles, _, tn = w4.shape
    dk = tn // GDN_HEADS
    body = functools.partial(_gdn_in_body, tiles_per_seq=tps, tm=tm, dk=dk)
    resident = pl.Buffered(1)
    return pl.pallas_call(
        body,
        grid=(m // tm,),
        in_specs=[
            pl.BlockSpec((tm, d), lambda i: (i, 0)),
            pl.BlockSpec((1, d), lambda i: (0, 0)),
            _mod_spec(0, tps, d),
            _mod_spec(1, tps, d),
            pl.BlockSpec((n_tiles, d, tn), lambda i: (0, 0, 0), pipeline_mode=resident),
            pl.BlockSpec((d, LANES), lambda i: (0, 0)),
            pl.BlockSpec((n_tiles, GDN_CONV, tn), lambda i: (0, 0, 0)),
            pl.BlockSpec((SUBLANES, LANES), lambda i: (0, 0)),
        ],
        out_specs=[
            pl.BlockSpec((n_tiles, tm, tn), lambda i: (0, i, 0)),
            pl.BlockSpec((tm, LANES), lambda i: (i, 0)),
        ],
        out_shape=[
            jax.ShapeDtypeStruct((n_tiles, m, tn), bf16),
            jax.ShapeDtypeStruct((m, LANES), f32),
        ],
        scratch_shapes=[
            pltpu.VMEM((tm, d), bf16),
            pltpu.VMEM((tm + SUBLANES, tn), f32),
            pltpu.VMEM((tm + SUBLANES, tn), f32),
            pltpu.VMEM((n_tiles - 1, SUBLANES, tn), f32),
        ],
        compiler_params=_params("arbitrary"),
        name="gdn_in_proj",
    )(x, norm_g, mod_l, mod_l, w4, wba, cw, gp)


def _gdn_core_body(q_ref, k_ref, v_ref, z_ref, ba_ref, ng_ref, o_ref,
                   s_scr, u_scr, w_scr, qd_scr, kt_scr, a_scr, gl_scr, *, tb, dk, gb):
    c_len = GDN_CHUNK
    n_chunks = tb // c_len
    heads = GDN_HEADS
    seqs = [(bi, h) for bi in range(gb) for h in range(heads)]
    hs = range(len(seqs))

    @pl.when(pl.program_id(1) == 0)
    def _():
        s_scr[...] = jnp.zeros(s_scr.shape, f32)

    row = lax.broadcasted_iota(jnp.int32, (c_len, c_len), 0)
    col = lax.broadcasted_iota(jnp.int32, (c_len, c_len), 1)
    causal = row >= col
    strict = row > col
    tril_f = causal.astype(f32)
    eye_f = (row == col).astype(f32)
    diag_blk = (row // GDN_SOLVE_BLOCK) == (col // GDN_SOLVE_BLOCK)

    def prep(cp, carry):
        x, rhs, decay, where = [], [], [], []
        for ci, bi in [(ci, bi) for ci in range(GDN_PREP_CHUNKS // gb) for bi in range(gb)]:
            c = cp * (GDN_PREP_CHUNKS // gb) + ci
            rows = pl.ds(pl.multiple_of(c * c_len, c_len), c_len)
            g8 = pl.multiple_of(c * SUBLANES, SUBLANES)
            ba = ba_ref[bi, rows, :]
            gc_all = jnp.dot(tril_f, ba, precision=lax.Precision.HIGHEST,
                             preferred_element_type=f32)
            gc_t = gc_all.T
            for h in range(heads):
                sl = slice(h * dk, (h + 1) * dk)
                slot = bi * heads + h
                q = q_ref[bi, rows, sl]
                k = k_ref[bi, rows, sl]
                beta = ba[:, h:h + 1]
                gc = gc_all[:, heads + h:heads + h + 1]
                gr = gc_t[heads + h:heads + h + 1, :]
                g_last = gc_all[c_len - 1:c_len, heads + h:heads + h + 1]
                dec = jnp.where(causal, jnp.exp(jnp.minimum(gc - gr, 0.0)), 0.0)
                eg = jnp.exp(gc)
                kf = k.astype(f32)
                kb = kf * beta
                kk = _dot_nt(kb.astype(bf16), k)
                x.append(jnp.where(strict, -(kk * dec), 0.0))
                rhs.append(jnp.concatenate([v_ref[bi, rows, sl].astype(f32) * beta, kb * eg], axis=-1))
                decay.append(dec)
                where.append((slot, bi, rows, sl))
                qd_scr[slot, rows, :] = (q.astype(f32) * eg).astype(bf16)
                kt_scr[slot, rows, :] = (kf * jnp.exp(g_last - gc)).astype(bf16)
                gl_scr[slot, pl.ds(g8, SUBLANES), :] = jnp.broadcast_to(jnp.exp(g_last), (SUBLANES, dk))
        items = range(len(x))
        for it in items:
            slot, bi, rows, sl = where[it]
            a_in = jnp.where(causal, _dot_nt(q_ref[bi, rows, sl], k_ref[bi, rows, sl]) * decay[it], 0.0)
            a_scr[slot, rows, :] = a_in.astype(bf16)
        x_d = [jnp.where(diag_blk, x[it], 0.0) for it in items]
        d_inv = [eye_f + x_d[it] for it in items]
        xp = x_d
        for _ in range(int(math.log2(GDN_SOLVE_BLOCK)) - 1):
            xp = [_dot_round(xp[it], xp[it]) for it in items]
            d_inv = [d_inv[it] + _dot_round(d_inv[it], xp[it]) for it in items]
        mn = [_dot_round(d_inv[it], x[it] - x_d[it]) for it in items]
        o_inv = [eye_f + mn[it] for it in items]
        mp = mn
        for _ in range(int(math.log2(c_len // GDN_SOLVE_BLOCK)) - 1):
            mp = [_dot_round(mp[it], mp[it]) for it in items]
            o_inv = [o_inv[it] + _dot_round(o_inv[it], mp[it]) for it in items]
        t_inv = [_dot_round(o_inv[it], d_inv[it]) for it in items]
        sol = [_dot_round(t_inv[it], rhs[it]) for it in items]
        for it in items:
            slot, _, rows, _ = where[it]
            u_scr[slot, rows, :] = sol[it][:, :dk]
            w_scr[slot, rows, :] = sol[it][:, dk:].astype(bf16)
        return carry

    lax.fori_loop(0, n_chunks * gb // GDN_PREP_CHUNKS, prep, 0)

    ng = ng_ref[...]

    def scan_chunk(c, s):
        rows = pl.ds(pl.multiple_of(c * c_len, c_len), c_len)
        g8 = pl.multiple_of(c * SUBLANES, SUBLANES)
        sb = [s[h].astype(bf16) for h in hs]
        ws = [_dot(w_scr[h, rows, :], sb[h]) for h in hs]
        qs = [_dot(qd_scr[h, rows, :], sb[h]) for h in hs]
        v_new = [(u_scr[h, rows, :] - ws[h]).astype(bf16) for h in hs]
        s_new = [s[h] * gl_scr[h, pl.ds(g8, 1), :] + _dot_tn(kt_scr[h, rows, :], v_new[h]) for h in hs]
        o = [qs[h] + _dot(a_scr[h, rows, :], v_new[h]) for h in hs]
        for slot, (bi, h) in enumerate(seqs):
            sl = slice(h * dk, (h + 1) * dk)
            z = z_ref[bi, rows, sl].astype(f32)
            ms = jnp.mean(o[slot] * o[slot], axis=-1, keepdims=True)
            o_ref[bi, rows, sl] = (o[slot] * lax.rsqrt(ms + EPS) * ng * _silu(z)).astype(bf16)
        return s_new

    def scan(cp, carry):
        s = [s_scr[h] for h in hs]
        for ci in range(GDN_SCAN_CHUNKS):
            s = scan_chunk(cp * GDN_SCAN_CHUNKS + ci, s)
        for h in hs:
            s_scr[h] = s[h]
        return carry

    lax.fori_loop(0, n_chunks // GDN_SCAN_CHUNKS, scan, 0)


def _gdn_core(qkvz, ba, norm_g, *, batch, seq):
    _, m, d = qkvz.shape
    dk = d // GDN_HEADS
    tb = GDN_TIME_BLOCK
    nt = seq // tb
    n_chunks = tb // GDN_CHUNK
    gb = GDN_BATCH_GROUP
    slots = gb * GDN_HEADS
    qkvz = qkvz.reshape(4, batch, seq, d)
    ba = ba.reshape(batch, seq, LANES)

    def col(jc):
        return pl.BlockSpec((None, gb, tb, d), lambda b, t: (jc, b, t, 0))

    body = functools.partial(_gdn_core_body, tb=tb, dk=dk, gb=gb)
    out = pl.pallas_call(
        body,
        grid=(batch // gb, nt),
        in_specs=[col(0), col(1), col(2), col(3),
                  pl.BlockSpec((gb, tb, LANES), lambda b, t: (b, t, 0)),
                  pl.BlockSpec((1, dk), lambda b, t: (0, 0))],
        out_specs=pl.BlockSpec((gb, tb, d), lambda b, t: (b, t, 0)),
        out_shape=jax.ShapeDtypeStruct((batch, seq, d), bf16),
        scratch_shapes=[
            pltpu.VMEM((slots, dk, dk), f32),
            pltpu.VMEM((slots, tb, dk), f32),
            pltpu.VMEM((slots, tb, dk), bf16),
            pltpu.VMEM((slots, tb, dk), bf16),
            pltpu.VMEM((slots, tb, dk), bf16),
            pltpu.VMEM((slots, tb, GDN_CHUNK), bf16),
            pltpu.VMEM((slots, n_chunks * SUBLANES, dk), f32),
        ],
        compiler_params=_params("arbitrary", "arbitrary"),
        name="gdn_core",
    )(qkvz, qkvz, qkvz, qkvz, ba, norm_g)
    return out.reshape(m, d)


def _out_proj_body(x_ref, o_ref, w_ref, g_ref, out_ref):
    out_ref[...] = x_ref[...] + g_ref[...] * _dot(o_ref[...], w_ref[...])


def _out_proj(x, o, w, mod_l, gate_idx, *, seq):
    m, d = x.shape
    tm = ROW_TILE
    tps = seq // tm
    return pl.pallas_call(
        _out_proj_body,
        grid=(m // tm,),
        in_specs=[
            pl.BlockSpec((tm, d), lambda i: (i, 0)),
            pl.BlockSpec((tm, d), lambda i: (i, 0)),
            pl.BlockSpec((d, d), lambda i: (0, 0)),
            _mod_spec(gate_idx, tps, d),
        ],
        out_specs=pl.BlockSpec((tm, d), lambda i: (i, 0)),
        out_shape=jax.ShapeDtypeStruct((m, d), f32),
        compiler_params=_params("arbitrary"),
        name="out_proj",
    )(x, o, w, mod_l)


def _dsw_in_body(x_ref, ng_ref, sh_ref, sc_ref, w_ref, gn_ref, cos_ref, sin_ref,
                 o0_ref, o1_ref, o2_ref, h_scr, ya_scr, yb_scr, slab_scr, slab2_scr, *, dh, tm, n_tiles):
    j = pl.program_id(1)
    y_scr = (ya_scr, yb_scr)
    out_refs = (o0_ref, o1_ref, o2_ref)

    def write_head(g, h, val):
        dil = DSW_PATTERNS[g][1]
        n = tm // dil
        sl = slice(h * dh, (h + 1) * dh)
        if dil == 1:
            out_refs[g][0, :, sl] = val.astype(bf16)
        elif dil <= DSW_MAX_LOAD_STRIDE:
            slab_scr[h] = val
            for r in range(dil):
                out_refs[g][r, :, sl] = slab_scr[h, pl.ds(r, n, stride=dil), :].astype(bf16)
        else:
            s1 = DSW_MAX_LOAD_STRIDE
            s2 = dil // s1
            n1 = tm // s1
            slab_scr[h] = val
            for r1 in range(s1):
                slab2_scr[h, r1 * n1:(r1 + 1) * n1, :] = slab_scr[h, pl.ds(r1, n1, stride=s1), :]
            for r1 in range(s1):
                for r2 in range(s2):
                    rows = pl.ds(r1 * n1 + r2, n, stride=s2)
                    out_refs[g][r1 + s1 * r2, :, sl] = slab2_scr[h, rows, :].astype(bf16)

    def epilogue(tile):
        g, which = divmod(tile, 3)
        y_ref = y_scr[tile % 2]
        if which == 2:
            for h in range(DSW_HEADS):
                write_head(g, h, y_ref[:, h * dh:(h + 1) * dh])
            return
        cos_t = cos_ref[...]
        sin_t = sin_ref[...]
        for h in range(DSW_HEADS):
            sl = slice(h * dh, (h + 1) * dh)
            yh = y_ref[:, sl]
            ms = jnp.mean(yh * yh, axis=-1, keepdims=True)
            yn = yh * lax.rsqrt(ms + EPS) * gn_ref[:, sl]
            write_head(g, h, yn * cos_t + pltpu.roll(yn, dh // 2, 1) * sin_t)

    for step in range(n_tiles + 1):
        @pl.when(j == step)
        def _(step=step):
            if step == 0:
                h = _norm_modulate(x_ref[...], ng_ref[...], sh_ref[...], sc_ref[...])
                h_scr[...] = h.astype(bf16)
            if step < n_tiles:
                y_scr[step % 2][...] = _dot(h_scr[...], w_ref[...])
            if step > 0:
                epilogue(step - 1)


def _dsw_in_proj(x, mod_l, norm_g, w, gn, cos_t, sin_t, *, batch, seq):
    m, d = x.shape
    tm = ROW_TILE
    tps = seq // tm
    n_tiles, _, tn = w.shape
    dh = d // DSW_HEADS
    body = functools.partial(_dsw_in_body, dh=dh, tm=tm, n_tiles=n_tiles)
    rope_spec = pl.BlockSpec((tm, dh), lambda i, j: (i % tps, 0))

    def out_spec(g):
        dil = DSW_PATTERNS[g][1]
        return pl.BlockSpec((None, dil, None, tm // dil, tn),
                            lambda i, j: (i // tps, 0, jnp.clip(j - 1 - 3 * g, 0, 2), i % tps, 0))

    groups = range(len(DSW_PATTERNS))
    return pl.pallas_call(
        body,
        grid=(m // tm, n_tiles + 1),
        in_specs=[
            pl.BlockSpec((tm, d), lambda i, j: (i, 0)),
            pl.BlockSpec((1, d), lambda i, j: (0, 0)),
            _mod_spec(0, tps, d),
            _mod_spec(1, tps, d),
            pl.BlockSpec((None, d, tn), lambda i, j: (jnp.minimum(j, n_tiles - 1), 0, 0)),
            pl.BlockSpec((None, 1, tn), lambda i, j: (jnp.maximum(j - 1, 0), 0, 0)),
            rope_spec, rope_spec,
        ],
        out_specs=[out_spec(g) for g in groups],
        out_shape=[jax.ShapeDtypeStruct((batch, DSW_PATTERNS[g][1], 3, seq // DSW_PATTERNS[g][1], d), bf16)
                   for g in groups],
        scratch_shapes=[pltpu.VMEM((tm, d), bf16), pltpu.VMEM((tm, tn), f32), pltpu.VMEM((tm, tn), f32),
                        pltpu.VMEM((DSW_HEADS, tm, dh), f32), pltpu.VMEM((DSW_HEADS, tm, dh), f32)],
        compiler_params=_params("arbitrary", "arbitrary"),
        name="dsw_in_proj",
    )(x, norm_g, mod_l, mod_l, w, gn, cos_t, sin_t)


def _dsw_attn_body(q_ref, kp_ref, kc_ref, vp_ref, vc_ref, o_ref, lse_ref, *, dh, span, nq):
    first_step = pl.program_id(2) == 0
    p_len = DSW_BLOCK
    qi = lax.broadcasted_iota(jnp.int32, (p_len, 2 * p_len), 0)
    ki = lax.broadcasted_iota(jnp.int32, (p_len, 2 * p_len), 1)
    rel = qi + p_len - ki
    band = (rel >= 0) & (rel <= span)
    lane = lax.broadcasted_iota(jnp.int32, (p_len, LANES), 1)
    hs = range(DSW_HEADS)

    def prev_and_cur(prev_ref, cur_ref, sb, sl):
        cur = cur_ref[sb * p_len:(sb + 1) * p_len, sl]
        prev = prev_ref[:, sl] if sb == 0 else cur_ref[(sb - 1) * p_len:sb * p_len, sl]
        return jnp.concatenate([prev, cur], axis=0)

    def block_scores(sb):
        return [_dot_nt(q_ref[sb * p_len:(sb + 1) * p_len, h * dh:(h + 1) * dh],
                        prev_and_cur(kp_ref, kc_ref, sb, slice(h * dh, (h + 1) * dh))) for h in hs]

    scores = block_scores(0)
    for sb in range(nq):
        nxt = block_scores(sb + 1) if sb + 1 < nq else None
        rows = slice(sb * p_len, (sb + 1) * p_len)
        valid = band & (jnp.logical_not(first_step) | (ki >= p_len)) if sb == 0 else band
        lse_tile = jnp.zeros((p_len, LANES), f32)
        for h in hs:
            sl = slice(h * dh, (h + 1) * dh)
            s = jnp.where(valid, scores[h], -jnp.inf)
            mx = jnp.max(s, axis=-1, keepdims=True)
            p = jnp.exp(s - mx)
            l = jnp.sum(p, axis=-1, keepdims=True)
            o = _dot(p.astype(bf16), prev_and_cur(vp_ref, vc_ref, sb, sl)) * (1.0 / l)
            o_ref[rows, sl] = o.astype(bf16)
            lse_tile = jnp.where(lane == h, mx + jnp.log(l), lse_tile)
        lse_ref[rows, :] = lse_tile
        scores = nxt


def _dsw_attention(qkv, window, dilation, *, d):
    batch, _, _, ts, _ = qkv.shape
    dh = d // DSW_HEADS
    nb = ts // DSW_BLOCK
    nq = min(DSW_BLOCKS_PER_STEP, nb)
    span = window // dilation

    def in_spec(which, prev):
        if prev:
            return pl.BlockSpec((None, None, None, DSW_BLOCK, d),
                                lambda b, r, s: (b, r, which, jnp.maximum(s * nq - 1, 0), 0))
        return pl.BlockSpec((None, None, None, nq * DSW_BLOCK, d), lambda b, r, s: (b, r, which, s, 0))

    body = functools.partial(_dsw_attn_body, dh=dh, span=span, nq=nq)
    return pl.pallas_call(
        body,
        grid=(batch, dilation, nb // nq),
        in_specs=[in_spec(0, False), in_spec(1, True), in_spec(1, False),
                  in_spec(2, True), in_spec(2, False)],
        out_specs=[
            pl.BlockSpec((None, None, nq * DSW_BLOCK, d), lambda b, r, s: (b, r, s, 0)),
            pl.BlockSpec((None, None, nq * DSW_BLOCK, LANES), lambda b, r, s: (b, r, s, 0)),
        ],
        out_shape=[
            jax.ShapeDtypeStruct((batch, dilation, ts, d), bf16),
            jax.ShapeDtypeStruct((batch, dilation, ts, LANES), f32),
        ],
        compiler_params=_params("arbitrary", "arbitrary", "arbitrary"),
        name=f"dsw_attn_d{dilation}",
    )(qkv, qkv, qkv, qkv, qkv)


def _dsw_out_body(x_ref, o0_ref, o1_ref, o2_ref, l0_ref, l1_ref, l2_ref, w_ref, g_ref,
                  out_ref, lse_scr, nat_scr, om_scr, *, dh, tm):
    o_refs = (o0_ref, o1_ref, o2_ref)
    l_refs = (l0_ref, l1_ref, l2_ref)
    groups = range(len(DSW_PATTERNS))

    def token_rows(g, r):
        dil = DSW_PATTERNS[g][1]
        return pl.ds(r, tm // dil, stride=dil)

    for g in groups:
        for r in range(DSW_PATTERNS[g][1]):
            lse_scr[g, token_rows(g, r), :] = l_refs[g][r]
    lse = [lse_scr[g] for g in groups]
    mx = jnp.maximum(jnp.maximum(lse[0], lse[1]), lse[2])
    e = [jnp.exp(lse[g] - mx) for g in groups]
    inv = 1.0 / (e[0] + e[1] + e[2])
    wts = [e[g] * inv for g in groups]
    for h in range(DSW_HEADS):
        sl = slice(h * dh, (h + 1) * dh)
        om = None
        for g in groups:
            for r in range(DSW_PATTERNS[g][1]):
                nat_scr[g, token_rows(g, r), :] = o_refs[g][r, :, sl].astype(f32)
            term = wts[g][:, h:h + 1] * nat_scr[g]
            om = term if om is None else om + term
        om_scr[:, sl] = om.astype(bf16)
    out_ref[...] = x_ref[...] + g_ref[...] * _dot(om_scr[...], w_ref[...])


def _dsw_out_proj(x, outs, lses, w, mod_l, *, seq):
    m, d = x.shape
    tm = DSW_OUT_ROW_TILE
    tps = seq // tm
    dh = d // DSW_HEADS
    row = pl.BlockSpec((tm, d), lambda i: (i, 0))

    def stream_spec(g, width):
        dil = DSW_PATTERNS[g][1]
        return pl.BlockSpec((None, dil, tm // dil, width), lambda i: (i // tps, 0, i % tps, 0))

    groups = range(len(DSW_PATTERNS))
    body = functools.partial(_dsw_out_body, dh=dh, tm=tm)
    return pl.pallas_call(
        body,
        grid=(m // tm,),
        in_specs=[row] + [stream_spec(g, d) for g in groups] + [stream_spec(g, LANES) for g in groups]
                 + [pl.BlockSpec((d, d), lambda i: (0, 0)), _mod_spec(2, tps, d)],
        out_specs=row,
        out_shape=jax.ShapeDtypeStruct((m, d), f32),
        scratch_shapes=[pltpu.VMEM((len(DSW_PATTERNS), tm, LANES), f32),
                        pltpu.VMEM((len(DSW_PATTERNS), tm, dh), f32),
                        pltpu.VMEM((tm, d), bf16)],
        compiler_params=_params("arbitrary"),
        name="dsw_out_proj",
    )(x, *outs, *lses, w, mod_l)


def _ffn_body(x_ref, ng_ref, sh_ref, sc_ref, g_ref, wgu_ref, wd_ref, out_ref, h_scr, *, th):
    hidden = wd_ref.shape[0]
    h = _norm_modulate(x_ref[...], ng_ref[...], sh_ref[...], sc_ref[...])
    h_scr[...] = h.astype(bf16)
    hb = h_scr[...]
    acc = None
    act_prev = None
    for t in range(hidden // th + 1):
        if t * th < hidden:
            gate = _dot(hb, wgu_ref[:, t * th:(t + 1) * th])
            up = _dot(hb, wgu_ref[:, hidden + t * th:hidden + (t + 1) * th])
        if act_prev is not None:
            part = _dot(act_prev, wd_ref[(t - 1) * th:t * th, :])
            acc = part if acc is None else acc + part
        if t * th < hidden:
            act_prev = (_silu(gate) * up).astype(bf16)
    out_ref[...] = x_ref[...] + g_ref[...] * acc


def _ffn(x, mod_l, norm_g, w_gate_up, w_down, *, seq):
    m, d = x.shape
    tm = ROW_TILE
    tps = seq // tm
    hidden = w_down.shape[0]
    resident = pl.Buffered(1)
    body = functools.partial(_ffn_body, th=FFN_HIDDEN_TILE)
    return pl.pallas_call(
        body,
        grid=(m // tm,),
        in_specs=[
            pl.BlockSpec((tm, d), lambda i: (i, 0)),
            pl.BlockSpec((1, d), lambda i: (0, 0)),
            _mod_spec(3, tps, d),
            _mod_spec(4, tps, d),
            _mod_spec(5, tps, d),
            pl.BlockSpec((d, 2 * hidden), lambda i: (0, 0), pipeline_mode=resident),
            pl.BlockSpec((hidden, d), lambda i: (0, 0), pipeline_mode=resident),
        ],
        out_specs=pl.BlockSpec((tm, d), lambda i: (i, 0)),
        out_shape=jax.ShapeDtypeStruct((m, d), f32),
        scratch_shapes=[pltpu.VMEM((tm, d), bf16)],
        compiler_params=_params("arbitrary"),
        name="ffn",
    )(x, norm_g, mod_l, mod_l, mod_l, w_gate_up, w_down)


def _rope_tables(seq, dh):
    rope_dim = dh // 4
    half = rope_dim // 2
    inv = jnp.exp(-math.log(ROPE_THETA) * (2.0 * jnp.arange(half, dtype=f32) / rope_dim))
    ang = jnp.arange(seq, dtype=f32)[:, None] * inv[None, :]
    cos = jnp.cos(ang)
    sin = jnp.sin(ang)
    ones = jnp.ones((seq, dh // 2 - half), f32)
    zeros = jnp.zeros((seq, dh // 2 - half), f32)
    cos_t = jnp.concatenate([cos, ones, cos, ones], axis=-1)
    sin_t = jnp.concatenate([-sin, zeros, sin, zeros], axis=-1)
    return cos_t, sin_t


def _rope_lane_order(t, dh):
    half = (dh // 4) // 2
    return jnp.concatenate([t[..., :half], t[..., 2 * half:dh // 2 + half],
                            t[..., half:2 * half], t[..., dh // 2 + half:]], axis=-1)


def kernel(x, c, mod_w, mod_b, mix_norm_g, ffn_norm_g, gdn_w_in, gdn_conv_w, gdn_A_log,
           gdn_dt_bias, gdn_norm_g, gdn_w_out, dsw_w_in, dsw_q_norm_g, dsw_k_norm_g,
           dsw_w_out, ffn_w_gate_up, ffn_w_down):
    batch, seq, d = x.shape
    depth = mod_w.shape[0]
    m = batch * seq
    heads = GDN_HEADS
    dh = d // DSW_HEADS
    assert seq % ROW_TILE == 0 and seq % GDN_TIME_BLOCK == 0 and d == heads * LANES
    assert batch % GDN_BATCH_GROUP == 0

    mod_rows = 16
    c_pad = jnp.zeros((mod_rows, d), f32).at[:batch].set(c)
    mod = _modulation(c_pad, mod_w, mod_b)[:, :batch]
    mod = mod.reshape(depth, batch, 6, 1, d)

    cos_t, sin_t = _rope_tables(seq, dh)
    xf = x.reshape(m, d)

    for layer in range(depth):
        mod_l = mod[layer]
        j = layer // N_MIXERS
        norm_g = mix_norm_g[layer].reshape(1, d)
        if layer % N_MIXERS == 0:
            n_conv = 3 * d
            w_in = gdn_w_in[j]
            w4 = w_in[:, :4 * d].reshape(d, 4, d).transpose(1, 0, 2).astype(bf16)
            wba = jnp.zeros((d, LANES), f32).at[:, :2 * heads].set(w_in[:, 4 * d:]).astype(bf16)
            cw = jnp.zeros((GDN_CONV, 4 * d), f32).at[:, :n_conv].set(gdn_conv_w[j])
            cw = cw.reshape(GDN_CONV, 4, d).transpose(1, 0, 2)
            gp = jnp.zeros((SUBLANES, LANES), f32)
            gp = gp.at[0, heads:2 * heads].set(gdn_A_log[j]).at[1, heads:2 * heads].set(gdn_dt_bias[j])
            qkvz, ba = _gdn_in_proj(xf, mod_l, norm_g, w4, wba, cw, gp, seq=seq)
            o = _gdn_core(qkvz, ba, gdn_norm_g[j].reshape(1, -1), batch=batch, seq=seq)
            xf = _out_proj(xf, o, gdn_w_out[j].astype(bf16), mod_l, 2, seq=seq)
        else:
            n_groups = len(DSW_PATTERNS)
            w5 = dsw_w_in[j].reshape(d, n_groups, 3, DSW_HEADS, dh)
            w5 = jnp.concatenate([_rope_lane_order(w5[:, :, :2], dh), w5[:, :, 2:]], axis=2)
            w_in = w5.reshape(d, 3 * n_groups, d).transpose(1, 0, 2).astype(bf16)
            gains = []
            for gi in range(n_groups):
                q_gain = _rope_lane_order(dsw_q_norm_g[j, gi], dh) * (dh ** -0.5)
                gains.append(jnp.tile(q_gain, DSW_HEADS))
                gains.append(jnp.tile(_rope_lane_order(dsw_k_norm_g[j, gi], dh), DSW_HEADS))
                gains.append(jnp.ones((d,), f32))
            gn = jnp.stack(gains).reshape(3 * n_groups, 1, d)
            qkvs = _dsw_in_proj(xf, mod_l, norm_g, w_in, gn, cos_t, sin_t, batch=batch, seq=seq)
            outs, lses = [], []
            for gi, (window, dilation) in enumerate(DSW_PATTERNS):
                o_g, lse_g = _dsw_attention(qkvs[gi], window, dilation, d=d)
                outs.append(o_g)
                lses.append(lse_g)
            xf = _dsw_out_proj(xf, outs, lses, dsw_w_out[j].astype(bf16), mod_l, seq=seq)
        xf = _ffn(xf, mod_l, ffn_norm_g[layer].reshape(1, d), ffn_w_gate_up[layer].astype(bf16),
                  ffn_w_down[layer].astype(bf16), seq=seq)
    return xf.reshape(batch, seq, d)
```

```python
import functools
import math

import jax
import jax.numpy as jnp
from jax import lax
from jax.experimental import pallas as pl
from jax.experimental.pallas import tpu as pltpu

EPS = 1e-6
N_MIXERS = 2
GDN_HEADS = 8
GDN_CONV = 4
GDN_CHUNK = 64
GDN_SOLVE_BLOCK = 16
GDN_BATCH_GROUP = 2
GDN_SCAN_CHUNKS = 2
GDN_PREP_CHUNKS = 4
DSW_PATTERNS = ((128, 1), (512, 4), (2048, 16))
DSW_HEADS = 8
DSW_BLOCK = 128
DSW_BLOCKS_PER_STEP = 4
ROPE_THETA = 500000.0

LANES = 128
SUBLANES = 8
VMEM_LIMIT_BYTES = 56 * 1024 * 1024

ROW_TILE = 1024
DSW_MAX_LOAD_STRIDE = 4
DSW_OUT_ROW_TILE = 512
GDN_TIME_BLOCK = 512
FFN_HIDDEN_TILE = 256

f32 = jnp.float32
bf16 = jnp.bfloat16


def _params(*sem):
    return pltpu.CompilerParams(dimension_semantics=sem, vmem_limit_bytes=VMEM_LIMIT_BYTES)


def _dot(a, b):
    return jnp.dot(a, b, preferred_element_type=f32)


def _dot_nt(a, b):
    return lax.dot_general(a, b, (((1,), (1,)), ((), ())), preferred_element_type=f32)


def _dot_tn(a, b):
    return lax.dot_general(a, b, (((0,), (0,)), ((), ())), preferred_element_type=f32)


def _dot_round(a, b):
    return _dot(a.astype(bf16), b.astype(bf16))


def _silu(x):
    return x * (0.5 + 0.5 * jnp.tanh(0.5 * x))


def _softplus(x):
    return jnp.maximum(x, 0.0) + jnp.log1p(jnp.exp(-jnp.abs(x)))


def _norm_modulate(x, gain, shift, scale):
    ms = jnp.mean(x * x, axis=-1, keepdims=True)
    y = x * lax.rsqrt(ms + EPS) * gain
    return y * (1.0 + scale) + shift


def _mod_body(c_ref, w_ref, b_ref, o_ref):
    c = c_ref[...]
    cond = _silu(c).astype(bf16)
    o_ref[...] = _dot(cond, w_ref[...].astype(bf16)) + b_ref[...]


def _modulation(c_pad, mod_w, mod_b):
    depth, d, n = mod_w.shape
    rows = c_pad.shape[0]
    tn = 1536
    return pl.pallas_call(
        _mod_body,
        grid=(depth, n // tn),
        in_specs=[
            pl.BlockSpec((rows, d), lambda l, j: (0, 0)),
            pl.BlockSpec((None, d, tn), lambda l, j: (l, 0, j)),
            pl.BlockSpec((None, 1, tn), lambda l, j: (l, 0, j)),
        ],
        out_specs=pl.BlockSpec((None, rows, tn), lambda l, j: (l, 0, j)),
        out_shape=jax.ShapeDtypeStruct((depth, rows, n), f32),
        compiler_params=_params("arbitrary", "arbitrary"),
        name="modulation",
    )(c_pad, mod_w, mod_b.reshape(depth, 1, n))


def _mod_spec(idx, tiles_per_seq, d):
    return pl.BlockSpec((None, None, 1, d), lambda i, *_: (i // tiles_per_seq, idx, 0, 0))


def _gdn_in_body(x_ref, ng_ref, sh_ref, sc_ref, w_ref, wba_ref, cw_ref, gp_ref,
                 out_ref, ba_ref, h_scr, ya_scr, yb_scr, tail_scr, *, tiles_per_seq, tm, dk):
    i = pl.program_id(0)
    n_tiles = w_ref.shape[0]
    y_scr = (ya_scr, yb_scr)
    first = (i % tiles_per_seq) == 0
    n_conv_tiles = n_tiles - 1

    @pl.when(i == 0)
    def _():
        tail_scr[...] = jnp.zeros(tail_scr.shape, f32)

    def epilogue(tile):
        y_ref = y_scr[tile % 2]
        if tile == n_conv_tiles:
            out_ref[tile] = y_ref[SUBLANES:, :].astype(bf16)
            return
        y_ref[0:SUBLANES, :] = jnp.where(first, 0.0, tail_scr[tile])
        tail_scr[tile] = y_ref[tm:, :]
        post = dk ** -0.5 if tile == 0 else 1.0
        for h in range(GDN_HEADS):
            sl = slice(h * dk, (h + 1) * dk)
            acc = cw_ref[tile, GDN_CONV - 1:GDN_CONV, sl] * y_ref[SUBLANES:, sl]
            for kk in range(GDN_CONV - 1):
                start = SUBLANES - (GDN_CONV - 1) + kk
                acc = acc + cw_ref[tile, kk:kk + 1, sl] * y_ref[pl.ds(start, tm), sl]
            a = _silu(acc)
            if tile < 2:
                ss = jnp.sum(a * a, axis=-1, keepdims=True)
                a = a * (lax.rsqrt(ss + EPS) * post)
            out_ref[tile, :, sl] = a.astype(bf16)

    def matmul(tile):
        y_scr[tile % 2][SUBLANES:, :] = _dot(h_scr[...], w_ref[tile])

    h = _norm_modulate(x_ref[...], ng_ref[...], sh_ref[...], sc_ref[...])
    hb = h.astype(bf16)
    h_scr[...] = hb
    matmul(0)
    ba = _dot(hb, wba_ref[...])
    lane = lax.broadcasted_iota(jnp.int32, ba.shape, 1)
    beta = jax.nn.sigmoid(ba)
    g = -jnp.exp(gp_ref[0:1, :]) * _softplus(ba + gp_ref[1:2, :])
    ba_ref[...] = jnp.where(lane < GDN_HEADS, beta, g)
    for tile in range(n_tiles):
        if tile + 1 < n_tiles:
            matmul(tile + 1)
        epilogue(tile)


def _gdn_in_proj(x, mod_l, norm_g, w4, wba, cw, gp, *, seq):
    m, d = x.shape
    tm = ROW_TILE
    tps = seq // tm
    n_tiles, _, tn = w4.shape
    dk = tn // GDN_HEADS
    body = functools.partial(_gdn_in_body, tiles_per_seq=tps, tm=tm, dk=dk)
    resident = pl.Buffered(1)
    return pl.pallas_call(
        body,
        grid=(m // tm,),
        in_specs=[
            pl.BlockSpec((tm, d), lambda i: (i, 0)),
            pl.BlockSpec((1, d), lambda i: (0, 0)),
            _mod_spec(0, tps, d),
            _mod_spec(1, tps, d),
            pl.BlockSpec((n_tiles, d, tn), lambda i: (0, 0, 0), pipeline_mode=resident),
            pl.BlockSpec((d, LANES), lambda i: (0, 0)),
            pl.BlockSpec((n_tiles, GDN_CONV, tn), lambda i: (0, 0, 0)),
            pl.BlockSpec((SUBLANES, LANES), lambda i: (0, 0)),
        ],
        out_specs=[
            pl.BlockSpec((n_tiles, tm, tn), lambda i: (0, i, 0)),
            pl.BlockSpec((tm, LANES), lambda i: (i, 0)),
        ],
        out_shape=[
            jax.ShapeDtypeStruct((n_tiles, m, tn), bf16),
            jax.ShapeDtypeStruct((m, LANES), f32),
        ],
        scratch_shapes=[
            pltpu.VMEM((tm, d), bf16),
            pltpu.VMEM((tm + SUBLANES, tn), f32),
            pltpu.VMEM((tm + SUBLANES, tn), f32),
            pltpu.VMEM((n_tiles - 1, SUBLANES, tn), f32),
        ],
        compiler_params=_params("arbitrary"),
        name="gdn_in_proj",
    )(x, norm_g, mod_l, mod_l, w4, wba, cw, gp)


def _gdn_core_body(q_ref, k_ref, v_ref, z_ref, ba_ref, ng_ref, o_ref,
                   s_scr, u_scr, w_scr, qd_scr, kt_scr, a_scr, gl_scr, *, tb, dk, gb):
    c_len = GDN_CHUNK
    n_chunks = tb // c_len
    heads = GDN_HEADS
    seqs = [(bi, h) for bi in range(gb) for h in range(heads)]
    hs = range(len(seqs))

    @pl.when(pl.program_id(1) == 0)
    def _():
        s_scr[...] = jnp.zeros(s_scr.shape, f32)

    row = lax.broadcasted_iota(jnp.int32, (c_len, c_len), 0)
    col = lax.broadcasted_iota(jnp.int32, (c_len, c_len), 1)
    causal = row >= col
    strict = row > col
    tril_f = causal.astype(f32)
    eye_f = (row == col).astype(f32)
    diag_blk = (row // GDN_SOLVE_BLOCK) == (col // GDN_SOLVE_BLOCK)

    def prep(cp, carry):
        x, rhs, decay, where = [], [], [], []
        for ci, bi in [(ci, bi) for ci in range(GDN_PREP_CHUNKS // gb) for bi in range(gb)]:
            c = cp * (GDN_PREP_CHUNKS // gb) + ci
            rows = pl.ds(pl.multiple_of(c * c_len, c_len), c_len)
            g8 = pl.multiple_of(c * SUBLANES, SUBLANES)
            ba = ba_ref[bi, rows, :]
            gc_all = jnp.dot(tril_f, ba, precision=lax.Precision.HIGHEST,
                             preferred_element_type=f32)
            gc_t = gc_all.T
            for h in range(heads):
                sl = slice(h * dk, (h + 1) * dk)
                slot = bi * heads + h
                q = q_ref[bi, rows, sl]
                k = k_ref[bi, rows, sl]
                beta = ba[:, h:h + 1]
                gc = gc_all[:, heads + h:heads + h + 1]
                gr = gc_t[heads + h:heads + h + 1, :]
                g_last = gc_all[c_len - 1:c_len, heads + h:heads + h + 1]
                dec = jnp.where(causal, jnp.exp(jnp.minimum(gc - gr, 0.0)), 0.0)
                eg = jnp.exp(gc)
                kf = k.astype(f32)
                kb = kf * beta
                kk = _dot_nt(kb.astype(bf16), k)
                x.append(jnp.where(strict, -(kk * dec), 0.0))
                rhs.append(jnp.concatenate([v_ref[bi, rows, sl].astype(f32) * beta, kb * eg], axis=-1))
                decay.append(dec)
                where.append((slot, bi, rows, sl))
                qd_scr[slot, rows, :] = (q.astype(f32) * eg).astype(bf16)
                kt_scr[slot, rows, :] = (kf * jnp.exp(g_last - gc)).astype(bf16)
                gl_scr[slot, pl.ds(g8, SUBLANES), :] = jnp.broadcast_to(jnp.exp(g_last), (SUBLANES, dk))
        items = range(len(x))
        for it in items:
            slot, bi, rows, sl = where[it]
            a_in = jnp.where(causal, _dot_nt(q_ref[bi, rows, sl], k_ref[bi, rows, sl]) * decay[it], 0.0)
            a_scr[slot, rows, :] = a_in.astype(bf16)
        x_d = [jnp.where(diag_blk, x[it], 0.0) for it in items]
        d_inv = [eye_f + x_d[it] for it in items]
        xp = x_d
        for _ in range(int(math.log2(GDN_SOLVE_BLOCK)) - 1):
            xp = [_dot_round(xp[it], xp[it]) for it in items]
            d_inv = [d_inv[it] + _dot_round(d_inv[it], xp[it]) for it in items]
        mn = [_dot_round(d_inv[it], x[it] - x_d[it]) for it in items]
        o_inv = [eye_f + mn[it] for it in items]
        mp = mn
        for _ in range(int(math.log2(c_len // GDN_SOLVE_BLOCK)) - 1):
            mp = [_dot_round(mp[it], mp[it]) for it in items]
            o_inv = [o_inv[it] + _dot_round(o_inv[it], mp[it]) for it in items]
        t_inv = [_dot_round(o_inv[it], d_inv[it]) for it in items]
        sol = [_dot_round(t_inv[it], rhs[it]) for it in items]
        for it in items:
            slot, _, rows, _ = where[it]
            u_scr[slot, rows, :] = sol[it][:, :dk]
            w_scr[slot, rows, :] = sol[it][:, dk:].astype(bf16)
        return carry

    lax.fori_loop(0, n_chunks * gb // GDN_PREP_CHUNKS, prep, 0)

    ng = ng_ref[...]

    def scan_chunk(c, s):
        rows = pl.ds(pl.multiple_of(c * c_len, c_len), c_len)
        g8 = pl.multiple_of(c * SUBLANES, SUBLANES)
        sb = [s[h].astype(bf16) for h in hs]
        ws = [_dot(w_scr[h, rows, :], sb[h]) for h in hs]
        qs = [_dot(qd_scr[h, rows, :], sb[h]) for h in hs]
        v_new = [(u_scr[h, rows, :] - ws[h]).astype(bf16) for h in hs]
        s_new = [s[h] * gl_scr[h, pl.ds(g8, 1), :] + _dot_tn(kt_scr[h, rows, :], v_new[h]) for h in hs]
        o = [qs[h] + _dot(a_scr[h, rows, :], v_new[h]) for h in hs]
        for slot, (bi, h) in enumerate(seqs):
            sl = slice(h * dk, (h + 1) * dk)
            z = z_ref[bi, rows, sl].astype(f32)
            ms = jnp.mean(o[slot] * o[slot], axis=-1, keepdims=True)
            o_ref[bi, rows, sl] = (o[slot] * lax.rsqrt(ms + EPS) * ng * _silu(z)).astype(bf16)
        return s_new

    def scan(cp, carry):
        s = [s_scr[h] for h in hs]
        for ci in range(GDN_SCAN_CHUNKS):
            s = scan_chunk(cp * GDN_SCAN_CHUNKS + ci, s)
        for h in hs:
            s_scr[h] = s[h]
        return carry

    lax.fori_loop(0, n_chunks // GDN_SCAN_CHUNKS, scan, 0)


def _gdn_core(qkvz, ba, norm_g, *, batch, seq):
    _, m, d = qkvz.shape
    dk = d // GDN_HEADS
    tb = GDN_TIME_BLOCK
    nt = seq // tb
    n_chunks = tb // GDN_CHUNK
    gb = GDN_BATCH_GROUP
    slots = gb * GDN_HEADS
    qkvz = qkvz.reshape(4, batch, seq, d)
    ba = ba.reshape(batch, seq, LANES)

    def col(jc):
        return pl.BlockSpec((None, gb, tb, d), lambda b, t: (jc, b, t, 0))

    body = functools.partial(_gdn_core_body, tb=tb, dk=dk, gb=gb)
    out = pl.pallas_call(
        body,
        grid=(batch // gb, nt),
        in_specs=[col(0), col(1), col(2), col(3),
                  pl.BlockSpec((gb, tb, LANES), lambda b, t: (b, t, 0)),
                  pl.BlockSpec((1, dk), lambda b, t: (0, 0))],
        out_specs=pl.BlockSpec((gb, tb, d), lambda b, t: (b, t, 0)),
        out_shape=jax.ShapeDtypeStruct((batch, seq, d), bf16),
        scratch_shapes=[
            pltpu.VMEM((slots, dk, dk), f32),
            pltpu.VMEM((slots, tb, dk), f32),
            pltpu.VMEM((slots, tb, dk), bf16),
            pltpu.VMEM((slots, tb, dk), bf16),
            pltpu.VMEM((slots, tb, dk), bf16),
            pltpu.VMEM((slots, tb, GDN_CHUNK), bf16),
            pltpu.VMEM((slots, n_chunks * SUBLANES, dk), f32),
        ],
        compiler_params=_params("arbitrary", "arbitrary"),
        name="gdn_core",
    )(qkvz, qkvz, qkvz, qkvz, ba, norm_g)
    return out.reshape(m, d)


def _dsw_in_body(x_ref, ng_ref, sh_ref, sc_ref, w_ref, gn_ref, cos_ref, sin_ref,
                 o0_ref, o1_ref, o2_ref, h_scr, ya_scr, yb_scr, slab_scr, slab2_scr, *, dh, tm, n_tiles):
    j = pl.program_id(1)
    y_scr = (ya_scr, yb_scr)
    out_refs = (o0_ref, o1_ref, o2_ref)

    def write_head(g, h, val):
        dil = DSW_PATTERNS[g][1]
        n = tm // dil
        sl = slice(h * dh, (h + 1) * dh)
        if dil == 1:
            out_refs[g][0, :, sl] = val.astype(bf16)
        elif dil <= DSW_MAX_LOAD_STRIDE:
            slab_scr[h] = val
            for r in range(dil):
                out_refs[g][r, :, sl] = slab_scr[h, pl.ds(r, n, stride=dil), :].astype(bf16)
        else:
            s1 = DSW_MAX_LOAD_STRIDE
            s2 = dil // s1
            n1 = tm // s1
            slab_scr[h] = val
            for r1 in range(s1):
                slab2_scr[h, r1 * n1:(r1 + 1) * n1, :] = slab_scr[h, pl.ds(r1, n1, stride=s1), :]
            for r1 in range(s1):
                for r2 in range(s2):
                    rows = pl.ds(r1 * n1 + r2, n, stride=s2)
                    out_refs[g][r1 + s1 * r2, :, sl] = slab2_scr[h, rows, :].astype(bf16)

    def epilogue(tile):
        g, which = divmod(tile, 3)
        y_ref = y_scr[tile % 2]
        if which == 2:
            for h in range(DSW_HEADS):
                write_head(g, h, y_ref[:, h * dh:(h + 1) * dh])
            return
        cos_t = cos_ref[...]
        sin_t = sin_ref[...]
        for h in range(DSW_HEADS):
            sl = slice(h * dh, (h + 1) * dh)
            yh = y_ref[:, sl]
            ms = jnp.mean(yh * yh, axis=-1, keepdims=True)
            yn = yh * lax.rsqrt(ms + EPS) * gn_ref[:, sl]
            write_head(g, h, yn * cos_t + pltpu.roll(yn, dh // 2, 1) * sin_t)

    for step in range(n_tiles + 1):
        @pl.when(j == step)
        def _(step=step):
            if step == 0:
                h = _norm_modulate(x_ref[...], ng_ref[...], sh_ref[...], sc_ref[...])
                h_scr[...] = h.astype(bf16)
            if step < n_tiles:
                y_scr[step % 2][...] = _dot(h_scr[...], w_ref[...])
            if step > 0:
                epilogue(step - 1)


def _dsw_in_proj(x, mod_l, norm_g, w, gn, cos_t, sin_t, *, batch, seq):
    m, d = x.shape
    tm = ROW_TILE
    tps = seq // tm
    n_tiles, _, tn = w.shape
    dh = d // DSW_HEADS
    body = functools.partial(_dsw_in_body, dh=dh, tm=tm, n_tiles=n_tiles)
    rope_spec = pl.BlockSpec((tm, dh), lambda i, j: (i % tps, 0))

    def out_spec(g):
        dil = DSW_PATTERNS[g][1]
        return pl.BlockSpec((None, dil, None, tm // dil, tn),
                            lambda i, j: (i // tps, 0, jnp.clip(j - 1 - 3 * g, 0, 2), i % tps, 0))

    groups = range(len(DSW_PATTERNS))
    return pl.pallas_call(
        body,
        grid=(m // tm, n_tiles + 1),
        in_specs=[
            pl.BlockSpec((tm, d), lambda i, j: (i, 0)),
            pl.BlockSpec((1, d), lambda i, j: (0, 0)),
            _mod_spec(0, tps, d),
            _mod_spec(1, tps, d),
            pl.BlockSpec((None, d, tn), lambda i, j: (jnp.minimum(j, n_tiles - 1), 0, 0)),
            pl.BlockSpec((None, 1, tn), lambda i, j: (jnp.maximum(j - 1, 0), 0, 0)),
            rope_spec, rope_spec,
        ],
        out_specs=[out_spec(g) for g in groups],
        out_shape=[jax.ShapeDtypeStruct((batch, DSW_PATTERNS[g][1], 3, seq // DSW_PATTERNS[g][1], d), bf16)
                   for g in groups],
        scratch_shapes=[pltpu.VMEM((tm, d), bf16), pltpu.VMEM((tm, tn), f32), pltpu.VMEM((tm, tn), f32),
                        pltpu.VMEM((DSW_HEADS, tm, dh), f32), pltpu.VMEM((DSW_HEADS, tm, dh), f32)],
        compiler_params=_params("arbitrary", "arbitrary"),
        name="dsw_in_proj",
    )(x, norm_g, mod_l, mod_l, w, gn, cos_t, sin_t)


def _dsw_attn_body(q_ref, kp_ref, kc_ref, vp_ref, vc_ref, o_ref, lse_ref, *, dh, span, nq):
    first_step = pl.program_id(2) == 0
    p_len = DSW_BLOCK
    qi = lax.broadcasted_iota(jnp.int32, (p_len, 2 * p_len), 0)
    ki = lax.broadcasted_iota(jnp.int32, (p_len, 2 * p_len), 1)
    rel = qi + p_len - ki
    band = (rel >= 0) & (rel <= span)
    lane = lax.broadcasted_iota(jnp.int32, (p_len, LANES), 1)
    hs = range(DSW_HEADS)

    def prev_and_cur(prev_ref, cur_ref, sb, sl):
        cur = cur_ref[sb * p_len:(sb + 1) * p_len, sl]
        prev = prev_ref[:, sl] if sb == 0 else cur_ref[(sb - 1) * p_len:sb * p_len, sl]
        return jnp.concatenate([prev, cur], axis=0)

    def block_scores(sb):
        return [_dot_nt(q_ref[sb * p_len:(sb + 1) * p_len, h * dh:(h + 1) * dh],
                        prev_and_cur(kp_ref, kc_ref, sb, slice(h * dh, (h + 1) * dh))) for h in hs]

    scores = block_scores(0)
    for sb in range(nq):
        nxt = block_scores(sb + 1) if sb + 1 < nq else None
        rows = slice(sb * p_len, (sb + 1) * p_len)
        valid = band & (jnp.logical_not(first_step) | (ki >= p_len)) if sb == 0 else band
        lse_tile = jnp.zeros((p_len, LANES), f32)
        for h in hs:
            sl = slice(h * dh, (h + 1) * dh)
            s = jnp.where(valid, scores[h], -jnp.inf)
            mx = jnp.max(s, axis=-1, keepdims=True)
            p = jnp.exp(s - mx)
            l = jnp.sum(p, axis=-1, keepdims=True)
            o = _dot(p.astype(bf16), prev_and_cur(vp_ref, vc_ref, sb, sl)) * (1.0 / l)
            o_ref[rows, sl] = o.astype(bf16)
            lse_tile = jnp.where(lane == h, mx + jnp.log(l), lse_tile)
        lse_ref[rows, :] = lse_tile
        scores = nxt


def _dsw_attention(qkv, window, dilation, *, d):
    batch, _, _, ts, _ = qkv.shape
    dh = d // DSW_HEADS
    nb = ts // DSW_BLOCK
    nq = min(DSW_BLOCKS_PER_STEP, nb)
    span = window // dilation

    def in_spec(which, prev):
        if prev:
            return pl.BlockSpec((None, None, None, DSW_BLOCK, d),
                                lambda b, r, s: (b, r, which, jnp.maximum(s * nq - 1, 0), 0))
        return pl.BlockSpec((None, None, None, nq * DSW_BLOCK, d), lambda b, r, s: (b, r, which, s, 0))

    body = functools.partial(_dsw_attn_body, dh=dh, span=span, nq=nq)
    return pl.pallas_call(
        body,
        grid=(batch, dilation, nb // nq),
        in_specs=[in_spec(0, False), in_spec(1, True), in_spec(1, False),
                  in_spec(2, True), in_spec(2, False)],
        out_specs=[
            pl.BlockSpec((None, None, nq * DSW_BLOCK, d), lambda b, r, s: (b, r, s, 0)),
            pl.BlockSpec((None, None, nq * DSW_BLOCK, LANES), lambda b, r, s: (b, r, s, 0)),
        ],
        out_shape=[
            jax.ShapeDtypeStruct((batch, dilation, ts, d), bf16),
            jax.ShapeDtypeStruct((batch, dilation, ts, LANES), f32),
        ],
        compiler_params=_params("arbitrary", "arbitrary", "arbitrary"),
        name=f"dsw_attn_d{dilation}",
    )(qkv, qkv, qkv, qkv, qkv)


def _dsw_out_body(x_ref, o0_ref, o1_ref, o2_ref, l0_ref, l1_ref, l2_ref, w_ref, g_ref,
                  out_ref, lse_scr, nat_scr, om_scr, *, dh, tm):
    o_refs = (o0_ref, o1_ref, o2_ref)
    l_refs = (l0_ref, l1_ref, l2_ref)
    groups = range(len(DSW_PATTERNS))

    def token_rows(g, r):
        dil = DSW_PATTERNS[g][1]
        return pl.ds(r, tm // dil, stride=dil)

    for g in groups:
        for r in range(DSW_PATTERNS[g][1]):
            lse_scr[g, token_rows(g, r), :] = l_refs[g][r]
    lse = [lse_scr[g] for g in groups]
    mx = jnp.maximum(jnp.maximum(lse[0], lse[1]), lse[2])
    e = [jnp.exp(lse[g] - mx) for g in groups]
    inv = 1.0 / (e[0] + e[1] + e[2])
    wts = [e[g] * inv for g in groups]
    for h in range(DSW_HEADS):
        sl = slice(h * dh, (h + 1) * dh)
        om = None
        for g in groups:
            for r in range(DSW_PATTERNS[g][1]):
                nat_scr[g, token_rows(g, r), :] = o_refs[g][r, :, sl].astype(f32)
            term = wts[g][:, h:h + 1] * nat_scr[g]
            om = term if om is None else om + term
        om_scr[:, sl] = om.astype(bf16)
    out_ref[...] = x_ref[...] + g_ref[...] * _dot(om_scr[...], w_ref[...])


def _dsw_out_proj(x, outs, lses, w, mod_l, *, seq):
    m, d = x.shape
    tm = DSW_OUT_ROW_TILE
    tps = seq // tm
    dh = d // DSW_HEADS
    row = pl.BlockSpec((tm, d), lambda i: (i, 0))

    def stream_spec(g, width):
        dil = DSW_PATTERNS[g][1]
        return pl.BlockSpec((None, dil, tm // dil, width), lambda i: (i // tps, 0, i % tps, 0))

    groups = range(len(DSW_PATTERNS))
    body = functools.partial(_dsw_out_body, dh=dh, tm=tm)
    return pl.pallas_call(
        body,
        grid=(m // tm,),
        in_specs=[row] + [stream_spec(g, d) for g in groups] + [stream_spec(g, LANES) for g in groups]
                 + [pl.BlockSpec((d, d), lambda i: (0, 0)), _mod_spec(2, tps, d)],
        out_specs=row,
        out_shape=jax.ShapeDtypeStruct((m, d), f32),
        scratch_shapes=[pltpu.VMEM((len(DSW_PATTERNS), tm, LANES), f32),
                        pltpu.VMEM((len(DSW_PATTERNS), tm, dh), f32),
                        pltpu.VMEM((tm, d), bf16)],
        compiler_params=_params("arbitrary"),
        name="dsw_out_proj",
    )(x, *outs, *lses, w, mod_l)


def _ffn_body(x_ref, ng_ref, sh_ref, sc_ref, g_ref, wgu_ref, wd_ref, *rest, th, mixer_out):
    hidden = wd_ref.shape[0]
    if mixer_out:
        o_ref, wo_ref, g1_ref, out_ref, h_scr = rest
        out_ref[...] = x_ref[...] + g1_ref[...] * _dot(o_ref[...], wo_ref[...])
        res_ref = out_ref
    else:
        out_ref, h_scr = rest
        res_ref = x_ref
    h = _norm_modulate(res_ref[...], ng_ref[...], sh_ref[...], sc_ref[...])
    h_scr[...] = h.astype(bf16)
    hb = h_scr[...]
    acc = None
    act_prev = None
    for t in range(hidden // th + 1):
        if t * th < hidden:
            gate = _dot(hb, wgu_ref[:, t * th:(t + 1) * th])
            up = _dot(hb, wgu_ref[:, hidden + t * th:hidden + (t + 1) * th])
        if act_prev is not None:
            part = _dot(act_prev, wd_ref[(t - 1) * th:t * th, :])
            acc = part if acc is None else acc + part
        if t * th < hidden:
            act_prev = (_silu(gate) * up).astype(bf16)
    out_ref[...] = res_ref[...] + g_ref[...] * acc


def _ffn(x, mod_l, norm_g, w_gate_up, w_down, *, seq, mixer_out=None):
    m, d = x.shape
    tm = ROW_TILE
    tps = seq // tm
    hidden = w_down.shape[0]
    resident = pl.Buffered(1)
    body = functools.partial(_ffn_body, th=FFN_HIDDEN_TILE, mixer_out=mixer_out is not None)
    in_specs = [
        pl.BlockSpec((tm, d), lambda i: (i, 0)),
        pl.BlockSpec((1, d), lambda i: (0, 0)),
        _mod_spec(3, tps, d),
        _mod_spec(4, tps, d),
        _mod_spec(5, tps, d),
        pl.BlockSpec((d, 2 * hidden), lambda i: (0, 0), pipeline_mode=resident),
        pl.BlockSpec((hidden, d), lambda i: (0, 0), pipeline_mode=resident),
    ]
    operands = [x, norm_g, mod_l, mod_l, mod_l, w_gate_up, w_down]
    if mixer_out is not None:
        in_specs += [pl.BlockSpec((tm, d), lambda i: (i, 0)),
                     pl.BlockSpec((d, d), lambda i: (0, 0), pipeline_mode=resident),
                     _mod_spec(2, tps, d)]
        operands += [mixer_out[0], mixer_out[1], mod_l]
    return pl.pallas_call(
        body,
        grid=(m // tm,),
        in_specs=in_specs,
        out_specs=pl.BlockSpec((tm, d), lambda i: (i, 0)),
        out_shape=jax.ShapeDtypeStruct((m, d), f32),
        scratch_shapes=[pltpu.VMEM((tm, d), bf16)],
        compiler_params=_params("arbitrary"),
        name="ffn_after_mixer" if mixer_out is not None else "ffn",
    )(*operands)


def _rope_tables(seq, dh):
    rope_dim = dh // 4
    half = rope_dim // 2
    inv = jnp.exp(-math.log(ROPE_THETA) * (2.0 * jnp.arange(half, dtype=f32) / rope_dim))
    ang = jnp.arange(seq, dtype=f32)[:, None] * inv[None, :]
    cos = jnp.cos(ang)
    sin = jnp.sin(ang)
    ones = jnp.ones((seq, dh // 2 - half), f32)
    zeros = jnp.zeros((seq, dh // 2 - half), f32)
    cos_t = jnp.concatenate([cos, ones, cos, ones], axis=-1)
    sin_t = jnp.concatenate([-sin, zeros, sin, zeros], axis=-1)
    return cos_t, sin_t


def _rope_lane_order(t, dh):
    half = (dh // 4) // 2
    return jnp.concatenate([t[..., :half], t[..., 2 * half:dh // 2 + half],
                            t[..., half:2 * half], t[..., dh // 2 + half:]], axis=-1)


def kernel(x, c, mod_w, mod_b, mix_norm_g, ffn_norm_g, gdn_w_in, gdn_conv_w, gdn_A_log,
           gdn_dt_bias, gdn_norm_g, gdn_w_out, dsw_w_in, dsw_q_norm_g, dsw_k_norm_g,
           dsw_w_out, ffn_w_gate_up, ffn_w_down):
    batch, seq, d = x.shape
    depth = mod_w.shape[0]
    m = batch * seq
    heads = GDN_HEADS
    dh = d // DSW_HEADS
    assert seq % ROW_TILE == 0 and seq % GDN_TIME_BLOCK == 0 and d == heads * LANES
    assert batch % GDN_BATCH_GROUP == 0

    mod_rows = 16
    c_pad = jnp.zeros((mod_rows, d), f32).at[:batch].set(c)
    mod = _modulation(c_pad, mod_w, mod_b)[:, :batch]
    mod = mod.reshape(depth, batch, 6, 1, d)

    cos_t, sin_t = _rope_tables(seq, dh)
    xf = x.reshape(m, d)

    for layer in range(depth):
        mod_l = mod[layer]
        j = layer // N_MIXERS
        norm_g = mix_norm_g[layer].reshape(1, d)
        if layer % N_MIXERS == 0:
            n_conv = 3 * d
            w_in = gdn_w_in[j]
            w4 = w_in[:, :4 * d].reshape(d, 4, d).transpose(1, 0, 2).astype(bf16)
            wba = jnp.zeros((d, LANES), f32).at[:, :2 * heads].set(w_in[:, 4 * d:]).astype(bf16)
            cw = jnp.zeros((GDN_CONV, 4 * d), f32).at[:, :n_conv].set(gdn_conv_w[j])
            cw = cw.reshape(GDN_CONV, 4, d).transpose(1, 0, 2)
            gp = jnp.zeros((SUBLANES, LANES), f32)
            gp = gp.at[0, heads:2 * heads].set(gdn_A_log[j]).at[1, heads:2 * heads].set(gdn_dt_bias[j])
            qkvz, ba = _gdn_in_proj(xf, mod_l, norm_g, w4, wba, cw, gp, seq=seq)
            o = _gdn_core(qkvz, ba, gdn_norm_g[j].reshape(1, -1), batch=batch, seq=seq)
            mixer_out = (o, gdn_w_out[j].astype(bf16))
        else:
            mixer_out = None
            n_groups = len(DSW_PATTERNS)
            w5 = dsw_w_in[j].reshape(d, n_groups, 3, DSW_HEADS, dh)
            w5 = jnp.concatenate([_rope_lane_order(w5[:, :, :2], dh), w5[:, :, 2:]], axis=2)
            w_in = w5.reshape(d, 3 * n_groups, d).transpose(1, 0, 2).astype(bf16)
            gains = []
            for gi in range(n_groups):
                q_gain = _rope_lane_order(dsw_q_norm_g[j, gi], dh) * (dh ** -0.5)
                gains.append(jnp.tile(q_gain, DSW_HEADS))
                gains.append(jnp.tile(_rope_lane_order(dsw_k_norm_g[j, gi], dh), DSW_HEADS))
                gains.append(jnp.ones((d,), f32))
            gn = jnp.stack(gains).reshape(3 * n_groups, 1, d)
            qkvs = _dsw_in_proj(xf, mod_l, norm_g, w_in, gn, cos_t, sin_t, batch=batch, seq=seq)
            outs, lses = [], []
            for gi, (window, dilation) in enumerate(DSW_PATTERNS):
                o_g, lse_g = _dsw_attention(qkvs[gi], window, dilation, d=d)
                outs.append(o_g)
                lses.append(lse_g)
            xf = _dsw_out_proj(xf, outs, lses, dsw_w_out[j].astype(bf16), mod_l, seq=seq)
        xf = _ffn(xf, mod_l, ffn_norm_g[layer].reshape(1, d), ffn_w_gate_up[layer].astype(bf16),
                  ffn_w_down[layer].astype(bf16), seq=seq, mixer_out=mixer_out)
    return xf.reshape(batch, seq, d)
```
